```python
import jax, jax.numpy as jnp
from jax import lax
import numpy as np

D_MODEL = 1024
BATCH = 8
SEQ = 2048
DEPTH = 2
DEC_BATCH = 16
DEC_SEQ = 16
PAST_LEN = 4096

CHUNK = 64
N_META = 16
D_CONV_A = D_MODEL
CONV_A_WIDTH = 31
D_CONV_B = D_MODEL
CONV_B_WIDTH = 3
D_HID = -(-8 * D_MODEL // (3 * 256)) * 256
IN_SPLITS = [D_CONV_A, D_CONV_A, D_CONV_B, D_CONV_B, D_CONV_B, D_MODEL, D_MODEL]
N_IN = sum(IN_SPLITS)
IN_OFFSETS = [int(o) for o in np.cumsum(IN_SPLITS[:-1])]
RMS_EPS = 1e-6
LN_EPS = 1e-5

kernel_name = "hybrid_conformer_shortconv_stream_step"


def rmsnorm(x, g):
    xf = x.astype(jnp.float32)
    y = xf * lax.rsqrt(jnp.mean(xf * xf, axis=-1, keepdims=True) + RMS_EPS)
    return (y * g.astype(jnp.float32)).astype(x.dtype)


def layernorm(x, g, b):
    xf = x.astype(jnp.float32)
    mu = jnp.mean(xf, axis=-1, keepdims=True)
    var = jnp.mean(jnp.square(xf - mu), axis=-1, keepdims=True)
    y = (xf - mu) * lax.rsqrt(var + LN_EPS)
    return (y * g.astype(jnp.float32) + b.astype(jnp.float32)).astype(x.dtype)


def causal_dwconv(x_full, w):
    return lax.conv_general_dilated(
        x_full, w[:, None, :].astype(x_full.dtype), window_strides=(1,), padding='VALID',
        dimension_numbers=('NWC', 'WIO', 'NWC'), feature_group_count=x_full.shape[-1])


def trunk_layer(x, st_a, st_b, norm1_g, w_in, conv_a_w, conv_a_b, ln_a_g, ln_a_b, w_a_out,
                conv_b_w, w_b_out, w_o, norm2_g, w_ffn_gate, w_ffn_up, w_ffn_down):
    xn = rmsnorm(x, norm1_g)
    proj = jnp.einsum('btd,dn->btn', xn, w_in)
    a_val, a_gate, b_b, b_c, b_x, g_a, g_b = jnp.split(proj, IN_OFFSETS, axis=-1)

    u_a = a_val * jax.nn.sigmoid(a_gate)
    u_a_full = jnp.concatenate([st_a.astype(u_a.dtype), u_a], axis=1)
    c_a = causal_dwconv(u_a_full, conv_a_w) + conv_a_b
    c_a = jax.nn.silu(layernorm(c_a, ln_a_g, ln_a_b))
    y_a = jnp.einsum('btc,cd->btd', c_a, w_a_out)

    z_b = b_c * b_x
    z_b_full = jnp.concatenate([st_b.astype(z_b.dtype), z_b], axis=1)
    c_b = causal_dwconv(z_b_full, conv_b_w)
    y_b = jnp.einsum('btc,cd->btd', b_b * c_b, w_b_out)

    merged = jax.nn.sigmoid(g_a) * y_a + jax.nn.sigmoid(g_b) * y_b
    h = x + jnp.einsum('btd,de->bte', merged, w_o)

    hn = rmsnorm(h, norm2_g)
    f = jax.nn.silu(jnp.einsum('btd,dh->bth', hn, w_ffn_gate)) * jnp.einsum('btd,dh->bth', hn, w_ffn_up)
    out = h + jnp.einsum('bth,hd->btd', f, w_ffn_down)

    new_a = u_a_full[:, -(CONV_A_WIDTH - 1):]
    new_b = z_b_full[:, -(CONV_B_WIDTH - 1):]
    return out, new_a, new_b


def setup_inputs(seed: int = 0) -> dict:
    key = jax.random.key(seed)
    ks = jax.random.split(key, 24)
    f32 = jnp.float32
    nrm = lambda k, shape, s: jax.random.normal(k, shape, f32) * s
    return {
        "x_prompt": nrm(ks[0], (BATCH, SEQ, D_MODEL), 1.0),
        "x_sample": nrm(ks[1], (DEC_BATCH, DEC_SEQ, D_MODEL), 1.0),
        "state_conv_a": nrm(ks[2], (DEPTH, DEC_BATCH, CONV_A_WIDTH - 1, D_CONV_A), 0.5),
        "state_conv_b": nrm(ks[3], (DEPTH, DEC_BATCH, CONV_B_WIDTH - 1, D_CONV_B), 0.5),
        "meta_tokens": nrm(ks[4], (N_META, D_MODEL), 1.0),
        "norm1_g": 1.0 + nrm(ks[5], (DEPTH, D_MODEL), 0.01),
        "w_in": nrm(ks[6], (DEPTH, D_MODEL, N_IN), D_MODEL ** -0.5),
        "conv_a_w": nrm(ks[7], (DEPTH, CONV_A_WIDTH, D_CONV_A), CONV_A_WIDTH ** -0.5),
        "conv_a_b": nrm(ks[8], (DEPTH, D_CONV_A), 0.01),
        "ln_a_g": 1.0 + nrm(ks[9], (DEPTH, D_CONV_A), 0.01),
        "ln_a_b": nrm(ks[10], (DEPTH, D_CONV_A), 0.01),
        "w_a_out": nrm(ks[11], (DEPTH, D_CONV_A, D_MODEL), D_CONV_A ** -0.5),
        "conv_b_w": nrm(ks[12], (DEPTH, CONV_B_WIDTH, D_CONV_B), CONV_B_WIDTH ** -0.5),
        "w_b_out": nrm(ks[13], (DEPTH, D_CONV_B, D_MODEL), D_CONV_B ** -0.5),
        "w_o": nrm(ks[14], (DEPTH, D_MODEL, D_MODEL), D_MODEL ** -0.5),
        "norm2_g": 1.0 + nrm(ks[15], (DEPTH, D_MODEL), 0.01),
        "w_ffn_gate": nrm(ks[16], (DEPTH, D_MODEL, D_HID), D_MODEL ** -0.5),
        "w_ffn_up": nrm(ks[17], (DEPTH, D_MODEL, D_HID), D_MODEL ** -0.5),
        "w_ffn_down": nrm(ks[18], (DEPTH, D_HID, D_MODEL), D_HID ** -0.5),
        "final_norm_g": 1.0 + nrm(ks[19], (D_MODEL,), 0.01),
    }


def reference(x_prompt, x_sample, state_conv_a, state_conv_b, meta_tokens, norm1_g, w_in,
              conv_a_w, conv_a_b, ln_a_g, ln_a_b, w_a_out, conv_b_w, w_b_out, w_o, norm2_g,
              w_ffn_gate, w_ffn_up, w_ffn_down, final_norm_g):
    b_p = x_prompt.shape[0]
    meta = jnp.broadcast_to(meta_tokens.astype(x_prompt.dtype)[None], (b_p, N_META, D_MODEL))
    hp = jnp.concatenate([meta, x_prompt], axis=1)
    zero_a = jnp.zeros((b_p, CONV_A_WIDTH - 1, D_CONV_A), x_prompt.dtype)
    zero_b = jnp.zeros((b_p, CONV_B_WIDTH - 1, D_CONV_B), x_prompt.dtype)
    hs = x_sample
    pa, pb, sa, sb = [], [], [], []
    for l in range(DEPTH):
        lw = (norm1_g[l], w_in[l], conv_a_w[l], conv_a_b[l], ln_a_g[l], ln_a_b[l], w_a_out[l],
              conv_b_w[l], w_b_out[l], w_o[l], norm2_g[l], w_ffn_gate[l], w_ffn_up[l], w_ffn_down[l])
        hp, na, nb = trunk_layer(hp, zero_a, zero_b, *lw)
        pa.append(na); pb.append(nb)
        hs, na, nb = trunk_layer(hs, state_conv_a[l], state_conv_b[l], *lw)
        sa.append(na); sb.append(nb)
    y_prompt = rmsnorm(hp, final_norm_g)[:, N_META:]
    y_sample = rmsnorm(hs, final_norm_g)
    new_conv_a_prompt = jnp.stack(pa, axis=0)
    new_conv_b_prompt = jnp.stack(pb, axis=0)
    new_conv_a_sample = jnp.stack(sa, axis=0)
    new_conv_b_sample = jnp.stack(sb, axis=0)
    return (y_prompt, y_sample, new_conv_a_prompt, new_conv_b_prompt, new_conv_a_sample, new_conv_b_sample)
```

```python
import functools

import jax
import jax.numpy as jnp
from jax import lax
from jax.experimental import pallas as pl
from jax.experimental.pallas import tpu as pltpu

F32 = jnp.float32
BF16 = jnp.bfloat16

D = 1024
D_HID = 2816
KA = 31
KB = 3
N_META = 16
RMS_EPS = 1e-6
LN_EPS = 1e-5

LANES = 128
SUBLANES = 8
BF16_ROWS = 16
N_SLABS = D // LANES

HA = 32
HB = 8
OFF_A = HA - (KA - 1)
OFF_B = HB - (KB - 1)
TM = 512
CONV_ROWS = 128
ROW_UNROLL = 4
VMEM_LIMIT = 60000 * 1024


def _sigmoid(x):
    return 0.5 * jnp.tanh(0.5 * x) + 0.5


def _silu(x):
    h = 0.5 * x
    return h * jnp.tanh(h) + h


def _rmsnorm(x, g):
    return x * lax.rsqrt(jnp.mean(x * x, axis=-1, keepdims=True) + RMS_EPS) * g


def _row_blocks(n_rows, body):
    def step(i, carry):
        body(pl.multiple_of(i * BF16_ROWS, BF16_ROWS))
        return carry
    lax.fori_loop(0, n_rows // BF16_ROWS, step, 0, unroll=ROW_UNROLL)


def _slab_loop(body):
    def step(j, carry):
        body(j, pl.ds(pl.multiple_of(j * LANES, LANES), LANES))
        return carry
    lax.fori_loop(0, N_SLABS, step, 0)


def _lane_slab(j):
    return slice(j * LANES, (j + 1) * LANES)


def _dot(a, b):
    return jnp.dot(a, b, preferred_element_type=F32)


def _from_slabs(buf, rs):
    return jnp.concatenate([buf[j, rs, :] for j in range(N_SLABS)], axis=-1)


def _conv_taps(load, store, w_ref, lanes, n_taps, off, rows, init):
    half = rows // 2
    for par in range(2):
        acc = init(half)
        for k in range(n_taps):
            acc = acc + load(par + k + off, half) * w_ref[k:k + 1, lanes]
        store(par, half, acc)


def _stage_rmsnorm_bf16(src_ref, g_ref, dst_ref, rows):
    def body(r):
        rs = pl.ds(r, BF16_ROWS)
        dst_ref[rs, :] = _rmsnorm(src_ref[rs, :], g_ref[...]).astype(BF16)
    _row_blocks(rows, body)


def _stage_layernorm_silu_bf16(c_buf, g_ref, b_ref, dst_ref, rows):
    def body(r):
        rs = pl.ds(r, BF16_ROWS)
        c = _from_slabs(c_buf, rs)
        mu = jnp.mean(c, axis=-1, keepdims=True)
        d = c - mu
        var = jnp.mean(d * d, axis=-1, keepdims=True)
        y = d * lax.rsqrt(var + LN_EPS) * g_ref[...] + b_ref[...]
        dst_ref[rs, :] = _silu(y).astype(BF16)
    _row_blocks(rows, body)


def _stage_gate_b_bf16(p_ref, c_buf, dst_ref, rows):
    def body(r):
        rs = pl.ds(r, BF16_ROWS)
        dst_ref[rs, :] = (p_ref[rs, 0:D] * _from_slabs(c_buf, rs)).astype(BF16)
    _row_blocks(rows, body)


def _stage_merge_bf16(g_ref, ya_ref, yb_ref, dst_ref, rows):
    def body(r):
        rs = pl.ds(r, BF16_ROWS)
        m = _sigmoid(g_ref[rs, 0:D]) * ya_ref[rs, :] + _sigmoid(g_ref[rs, D:2 * D]) * yb_ref[rs, :]
        dst_ref[rs, :] = m.astype(BF16)
    _row_blocks(rows, body)


def _stage_swiglu_bf16(g_ref, u_ref, dst_ref, rows):
    chunk = 2 * LANES
    def body(r):
        rs = pl.ds(r, BF16_ROWS)
        for c in range(0, D_HID, chunk):
            g = g_ref[rs, c:c + chunk]
            dst_ref[rs, c:c + chunk] = (_silu(g) * u_ref[rs, c:c + chunk]).astype(BF16)
    _row_blocks(rows, body)


def _stage_final_norm_inplace(ref, g_ref, rows):
    def body(r):
        rs = pl.ds(r, BF16_ROWS)
        ref[rs, :] = _rmsnorm(ref[rs, :], g_ref[...])
    _row_blocks(rows, body)


def _mixer_kernel(x_ref, ha0_ref, hb0_ref, n1g_ref, caw_ref, cab_ref, lng_ref, lnb_ref, cbw_ref,
                  w_in_ref, w_a_ref, w_b_ref, w_o_ref,
                  h_ref, na_ref, nb_ref,
                  ua_buf, zb_buf, c_buf, xn_buf, lhs_buf, p_buf, ya_buf, yb_buf):
    t = pl.program_id(1)
    last_t = pl.num_programs(1) - 1

    @pl.when(t == 0)
    def _():
        for j in range(N_SLABS):
            ua_buf[j, 0:HA, :] = ha0_ref[:, _lane_slab(j)]
            zb_buf[j, 0:HB, :] = hb0_ref[:, _lane_slab(j)]

    _stage_rmsnorm_bf16(x_ref, n1g_ref, xn_buf, TM)

    p_buf[:, 0:2 * D] = _dot(xn_buf[...], w_in_ref[:, 0:2 * D])

    def glu(r):
        rs = pl.ds(r, BF16_ROWS)
        u = p_buf[rs, 0:D] * _sigmoid(p_buf[rs, D:2 * D])
        for j in range(N_SLABS):
            ua_buf[j, pl.ds(HA + r, BF16_ROWS), :] = u[:, _lane_slab(j)]
    _row_blocks(TM, glu)

    def conv_a(j, lanes):
        for r0 in range(0, TM, CONV_ROWS):
            def load(start, n, r0=r0):
                return ua_buf[j, pl.ds(r0 + start, n, stride=2), :]
            def store(start, n, v, r0=r0):
                c_buf[j, pl.ds(r0 + start, n, stride=2), :] = v
            _conv_taps(load, store, caw_ref, lanes, KA, OFF_A, CONV_ROWS,
                       lambda n: jnp.broadcast_to(cab_ref[:, lanes], (n, LANES)))
    _slab_loop(conv_a)

    @pl.when(t == last_t)
    def _():
        for j in range(N_SLABS):
            na_ref[:, _lane_slab(j)] = ua_buf[j, TM + OFF_A:TM + HA, :]
    ua_buf[:, 0:HA, :] = ua_buf[:, TM:TM + HA, :]

    _stage_layernorm_silu_bf16(c_buf, lng_ref, lnb_ref, lhs_buf, TM)
    ya_buf[...] = _dot(lhs_buf[...], w_a_ref[...])

    p_buf[...] = _dot(xn_buf[...], w_in_ref[:, 2 * D:5 * D])

    def zb(r):
        rs = pl.ds(r, BF16_ROWS)
        z = p_buf[rs, D:2 * D] * p_buf[rs, 2 * D:3 * D]
        for j in range(N_SLABS):
            zb_buf[j, pl.ds(HB + r, BF16_ROWS), :] = z[:, _lane_slab(j)]
    _row_blocks(TM, zb)

    def conv_b(j, lanes):
        for r0 in range(0, TM, CONV_ROWS):
            def load(start, n, r0=r0):
                return zb_buf[j, pl.ds(r0 + start, n, stride=2), :]
            def store(start, n, v, r0=r0):
                c_buf[j, pl.ds(r0 + start, n, stride=2), :] = v
            _conv_taps(load, store, cbw_ref, lanes, KB, OFF_B, CONV_ROWS,
                       lambda n: jnp.zeros((n, LANES), F32))
    _slab_loop(conv_b)

    @pl.when(t == last_t)
    def _():
        for j in range(N_SLABS):
            nb_ref[:, _lane_slab(j)] = zb_buf[j, TM + OFF_B:TM + HB, :]
    zb_buf[:, 0:HB, :] = zb_buf[:, TM:TM + HB, :]

    _stage_gate_b_bf16(p_buf, c_buf, lhs_buf, TM)
    yb_buf[...] = _dot(lhs_buf[...], w_b_ref[...])

    p_buf[:, 0:2 * D] = _dot(xn_buf[...], w_in_ref[:, 5 * D:7 * D])
    _stage_merge_bf16(p_buf, ya_buf, yb_buf, lhs_buf, TM)
    h_ref[...] = x_ref[...] + _dot(lhs_buf[...], w_o_ref[...])


def _vmem_spec():
    return pl.BlockSpec(memory_space=pltpu.VMEM)


def _prompt_mixer(x, ha0, hb0, lw):
    b, s, _ = x.shape
    row_spec = pl.BlockSpec((None, TM, D), lambda i, j: (i, j, 0))
    return pl.pallas_call(
        _mixer_kernel,
        grid=(b, s // TM),
        in_specs=[row_spec] + [_vmem_spec()] * 12,
        out_specs=[row_spec,
                   pl.BlockSpec((None, KA - 1, D), lambda i, j: (i, 0, 0)),
                   pl.BlockSpec((None, KB - 1, D), lambda i, j: (i, 0, 0))],
        out_shape=[jax.ShapeDtypeStruct((b, s, D), F32),
                   jax.ShapeDtypeStruct((b, KA - 1, D), F32),
                   jax.ShapeDtypeStruct((b, KB - 1, D), F32)],
        scratch_shapes=[pltpu.VMEM((N_SLABS, HA + TM, LANES), F32),
                        pltpu.VMEM((N_SLABS, HB + TM, LANES), F32),
                        pltpu.VMEM((N_SLABS, TM, LANES), F32),
                        pltpu.VMEM((TM, D), BF16),
                        pltpu.VMEM((TM, D), BF16),
                        pltpu.VMEM((TM, 3 * D), F32),
                        pltpu.VMEM((TM, D), F32),
                        pltpu.VMEM((TM, D), F32)],
        compiler_params=pltpu.CompilerParams(
            dimension_semantics=("arbitrary", "arbitrary"), vmem_limit_bytes=VMEM_LIMIT),
        name="prompt_mixer",
    )(x, ha0, hb0, lw["norm1_g"], lw["conv_a_w"], lw["conv_a_b"], lw["ln_a_g"], lw["ln_a_b"],
      lw["conv_b_w"], lw["w_in"], lw["w_a_out"], lw["w_b_out"], lw["w_o"])


def _ffn_kernel(final_norm, h_ref, n2g_ref, wg_ref, wu_ref, wd_ref, fg_ref, out_ref,
                hn_buf, g_buf, u_buf, f_buf):
    rows = h_ref.shape[0]
    _stage_rmsnorm_bf16(h_ref, n2g_ref, hn_buf, rows)
    g_buf[...] = _dot(hn_buf[...], wg_ref[...])
    u_buf[...] = _dot(hn_buf[...], wu_ref[...])
    _stage_swiglu_bf16(g_buf, u_buf, f_buf, rows)
    out_ref[...] = h_ref[...] + _dot(f_buf[...], wd_ref[...])
    if final_norm:
        _stage_final_norm_inplace(out_ref, fg_ref, rows)


def _ffn_call(h, lw, final_g, final_norm, tm, name):
    n = h.shape[0]
    row_spec = pl.BlockSpec((tm, D), lambda i: (i, 0))
    return pl.pallas_call(
        functools.partial(_ffn_kernel, final_norm),
        grid=(n // tm,),
        in_specs=[row_spec] + [_vmem_spec()] * 5,
        out_specs=row_spec,
        out_shape=jax.ShapeDtypeStruct((n, D), F32),
        scratch_shapes=[pltpu.VMEM((tm, D), BF16),
                        pltpu.VMEM((tm, D_HID), F32),
                        pltpu.VMEM((tm, D_HID), F32),
                        pltpu.VMEM((tm, D_HID), BF16)],
        compiler_params=pltpu.CompilerParams(
            dimension_semantics=("arbitrary",), vmem_limit_bytes=VMEM_LIMIT),
        name=name,
    )(h, lw["norm2_g"], lw["w_ffn_gate"], lw["w_ffn_up"], lw["w_ffn_down"], final_g)


def _small_kernel(n_seq, seq_len,
                  x_ref, ha_ref, hb_ref, n1g_ref, caw_ref, cab_ref, lng_ref, lnb_ref, cbw_ref,
                  w_in_ref, w_a_ref, w_b_ref, w_o_ref,
                  h_ref, na_ref, nb_ref,
                  ua_buf, zb_buf, c_buf, xn_buf, lhs_buf, p_buf, ya_buf, yb_buf):
    rows = n_seq * seq_len
    for j in range(N_SLABS):
        ua_buf[:, j, 0:HA, :] = ha_ref[:, :, _lane_slab(j)]
        zb_buf[:, j, 0:HB, :] = hb_ref[:, :, _lane_slab(j)]

    _stage_rmsnorm_bf16(x_ref, n1g_ref, xn_buf, rows)

    p_buf[:, 0:2 * D] = _dot(xn_buf[...], w_in_ref[:, 0:2 * D])
    for s in range(n_seq):
        rs = slice(s * seq_len, (s + 1) * seq_len)
        u = p_buf[rs, 0:D] * _sigmoid(p_buf[rs, D:2 * D])
        for j in range(N_SLABS):
            ua_buf[s, j, HA:HA + seq_len, :] = u[:, _lane_slab(j)]

    def conv_a(j, lanes):
        for s in range(n_seq):
            def load(start, n, s=s):
                return ua_buf[s, j, pl.ds(start, n, stride=2), :]
            def store(start, n, v, s=s):
                c_buf[j, pl.ds(s * seq_len + start, n, stride=2), :] = v
            _conv_taps(load, store, caw_ref, lanes, KA, OFF_A, seq_len,
                       lambda n: jnp.broadcast_to(cab_ref[:, lanes], (n, LANES)))
    _slab_loop(conv_a)
    for j in range(N_SLABS):
        na_ref[:, :, _lane_slab(j)] = ua_buf[:, j, seq_len + OFF_A:seq_len + HA, :]

    _stage_layernorm_silu_bf16(c_buf, lng_ref, lnb_ref, lhs_buf, rows)
    ya_buf[...] = _dot(lhs_buf[...], w_a_ref[...])

    p_buf[...] = _dot(xn_buf[...], w_in_ref[:, 2 * D:5 * D])
    for s in range(n_seq):
        rs = slice(s * seq_len, (s + 1) * seq_len)
        z = p_buf[rs, D:2 * D] * p_buf[rs, 2 * D:3 * D]
        for j in range(N_SLABS):
            zb_buf[s, j, HB:HB + seq_len, :] = z[:, _lane_slab(j)]

    def conv_b(j, lanes):
        for s in range(n_seq):
            def load(start, n, s=s):
                return zb_buf[s, j, pl.ds(start, n, stride=2), :]
            def store(start, n, v, s=s):
                c_buf[j, pl.ds(s * seq_len + start, n, stride=2), :] = v
            _conv_taps(load, store, cbw_ref, lanes, KB, OFF_B, seq_len,
                       lambda n: jnp.zeros((n, LANES), F32))
    _slab_loop(conv_b)
    for j in range(N_SLABS):
        nb_ref[:, :, _lane_slab(j)] = zb_buf[:, j, seq_len + OFF_B:seq_len + HB, :]

    _stage_gate_b_bf16(p_buf, c_buf, lhs_buf, rows)
    yb_buf[...] = _dot(lhs_buf[...], w_b_ref[...])

    p_buf[:, 0:2 * D] = _dot(xn_buf[...], w_in_ref[:, 5 * D:7 * D])
    _stage_merge_bf16(p_buf, ya_buf, yb_buf, lhs_buf, rows)
    h_ref[...] = x_ref[...] + _dot(lhs_buf[...], w_o_ref[...])


def _small_mixer(x, ha, hb, lw):
    n_seq = ha.shape[0]
    rows = x.shape[0]
    seq_len = rows // n_seq
    return pl.pallas_call(
        functools.partial(_small_kernel, n_seq, seq_len),
        in_specs=[_vmem_spec()] * 13,
        out_specs=[_vmem_spec()] * 3,
        out_shape=[jax.ShapeDtypeStruct((rows, D), F32),
                   jax.ShapeDtypeStruct((n_seq, KA - 1, D), F32),
                   jax.ShapeDtypeStruct((n_seq, KB - 1, D), F32)],
        scratch_shapes=[pltpu.VMEM((n_seq, N_SLABS, HA + seq_len, LANES), F32),
                        pltpu.VMEM((n_seq, N_SLABS, HB + seq_len, LANES), F32),
                        pltpu.VMEM((N_SLABS, rows, LANES), F32),
                        pltpu.VMEM((rows, D), BF16),
                        pltpu.VMEM((rows, D), BF16),
                        pltpu.VMEM((rows, 3 * D), F32),
                        pltpu.VMEM((rows, D), F32),
                        pltpu.VMEM((rows, D), F32)],
        compiler_params=pltpu.CompilerParams(vmem_limit_bytes=VMEM_LIMIT),
        name="small_mixer",
    )(x, ha, hb, lw["norm1_g"], lw["conv_a_w"], lw["conv_a_b"], lw["ln_a_g"], lw["ln_a_b"],
      lw["conv_b_w"], lw["w_in"], lw["w_a_out"], lw["w_b_out"], lw["w_o"])


def _front_pad(a, n_rows):
    pad = [(0, 0)] * a.ndim
    pad[-2] = (n_rows - a.shape[-2], 0)
    return jnp.pad(a, pad)


def kernel(x_prompt, x_sample, state_conv_a, state_conv_b, meta_tokens, norm1_g, w_in, conv_a_w, conv_a_b, ln_a_g, ln_a_b, w_a_out, conv_b_w, w_b_out, w_o, norm2_g, w_ffn_gate, w_ffn_up, w_ffn_down, final_norm_g):
    depth = w_in.shape[0]
    b_p, seq, _ = x_prompt.shape
    b_s, seq_s, _ = x_sample.shape
    assert seq % TM == 0 and seq_s == N_META and seq_s % BF16_ROWS == 0

    row = lambda v: v.reshape(1, -1).astype(F32)
    final_g = row(final_norm_g)
    layers = []
    for l in range(depth):
        layers.append(dict(
            norm1_g=row(norm1_g[l]), conv_a_w=conv_a_w[l], conv_a_b=row(conv_a_b[l]),
            ln_a_g=row(ln_a_g[l]), ln_a_b=row(ln_a_b[l]), conv_b_w=conv_b_w[l],
            w_in=w_in[l].astype(BF16), w_a_out=w_a_out[l].astype(BF16),
            w_b_out=w_b_out[l].astype(BF16), w_o=w_o[l].astype(BF16),
            norm2_g=row(norm2_g[l]), w_ffn_gate=w_ffn_gate[l].astype(BF16),
            w_ffn_up=w_ffn_up[l].astype(BF16), w_ffn_down=w_ffn_down[l].astype(BF16)))

    xs = jnp.concatenate([meta_tokens.astype(F32), x_sample.reshape(b_s * seq_s, D)], axis=0)
    xp = x_prompt
    pa, pb, sa, sb = [], [], [], []
    for l in range(depth):
        last = l == depth - 1
        lw = layers[l]
        ha = _front_pad(jnp.concatenate([jnp.zeros((1, KA - 1, D), F32), state_conv_a[l]], axis=0), HA)
        hb = _front_pad(jnp.concatenate([jnp.zeros((1, KB - 1, D), F32), state_conv_b[l]], axis=0), HB)
        hs, na_s, nb_s = _small_mixer(xs, ha, hb, lw)
        xs = _ffn_call(hs, lw, final_g, last, hs.shape[0], "small_ffn")
        sa.append(na_s[1:])
        sb.append(nb_s[1:])
        h, na_p, nb_p = _prompt_mixer(xp, _front_pad(na_s[0], HA), _front_pad(nb_s[0], HB), lw)
        pa.append(na_p)
        pb.append(nb_p)
        xp = _ffn_call(h.reshape(b_p * seq, D), lw, final_g, last, TM, "prompt_ffn").reshape(b_p, seq, D)

    y_prompt = xp
    y_sample = xs[N_META:].reshape(b_s, seq_s, D)
    return (y_prompt, y_sample, jnp.stack(pa, axis=0), jnp.stack(pb, axis=0),
            jnp.stack(sa, axis=0), jnp.stack(sb, axis=0))
```

```python
import functools

import jax
import jax.numpy as jnp
from jax import lax
from jax.experimental import pallas as pl
from jax.experimental.pallas import tpu as pltpu

F32 = jnp.float32
BF16 = jnp.bfloat16

D = 1024
D_HID = 2816
KA = 31
KB = 3
N_META = 16
RMS_EPS = 1e-6
LN_EPS = 1e-5

LANES = 128
BF16_ROWS = 16
N_SLABS = D // LANES
MXU_COLS = 256
SLABS_PER_CHUNK = MXU_COLS // LANES
N_CHUNKS = D // MXU_COLS

HA = 32
HB = 8
OFF_A = HA - (KA - 1)
OFF_B = HB - (KB - 1)
TM = 512
CONV_ROWS = 64
ROW_UNROLL = 4
VMEM_LIMIT = 60000 * 1024

N_GROUPS = 7
MID0 = 2 * D
MID_CHUNK = 5 * MXU_COLS


def _projection_block_order():
    val, gate, bb, bc, bx, ga, gb = (g * N_SLABS for g in range(N_GROUPS))
    order = []
    for j in range(N_SLABS):
        order += [val + j, gate + j]
    for i in range(N_CHUNKS):
        j0, j1 = SLABS_PER_CHUNK * i, SLABS_PER_CHUNK * i + 1
        order += [bb + j0, bb + j1, bc + j0, bx + j0, bc + j1, bx + j1, ga + j0, gb + j0, ga + j1, gb + j1]
    return order


def _glu_col(j):
    return j * MXU_COLS


def _mid_col(j, pair):
    jj = j % SLABS_PER_CHUNK
    return (jj * LANES, MXU_COLS + jj * MXU_COLS, 3 * MXU_COLS + jj * MXU_COLS)[pair]


def _mid_abs_col(j, pair):
    return MID0 + (j // SLABS_PER_CHUNK) * MID_CHUNK + _mid_col(j, pair)


def _sigmoid(x):
    return 0.5 * jnp.tanh(0.5 * x) + 0.5


def _silu(x):
    h = 0.5 * x
    return h * jnp.tanh(h) + h


def _rmsnorm(x, g):
    return x * lax.rsqrt(jnp.mean(x * x, axis=-1, keepdims=True) + RMS_EPS) * g


def _layernorm_silu(c, g, b):
    mu = jnp.mean(c, axis=-1, keepdims=True)
    d = c - mu
    var = jnp.mean(d * d, axis=-1, keepdims=True)
    return _silu(d * lax.rsqrt(var + LN_EPS) * g + b)


def _row_blocks(n_rows, body):
    def step(i, carry):
        body(pl.multiple_of(i * BF16_ROWS, BF16_ROWS))
        return carry
    lax.fori_loop(0, n_rows // BF16_ROWS, step, 0, unroll=ROW_UNROLL)


def _slab_loop(body):
    def step(j, carry):
        body(j, pl.ds(pl.multiple_of(j * LANES, LANES), LANES))
        return carry
    lax.fori_loop(0, N_SLABS, step, 0)


def _lane_slab(j):
    return slice(j * LANES, (j + 1) * LANES)


def _dot(a, b):
    return jnp.dot(a, b, preferred_element_type=F32)


def _from_slabs(buf, rs):
    return jnp.concatenate([buf[j, rs, :] for j in range(N_SLABS)], axis=-1)


def _zero_after(v):
    bits = pltpu.bitcast(v, jnp.uint32)
    bits = lax.shift_right_logical(lax.shift_right_logical(bits, jnp.uint32(16)), jnp.uint32(16))
    return pltpu.bitcast(bits, F32)


def _conv_taps(load, store, w_ref, lanes, n_taps, off, rows, init, post=None, after=None):
    half = rows // 2
    acc = after
    for par in range(2):
        acc = init(half) if acc is None else init(half) + _zero_after(acc)
        for k in range(n_taps):
            acc = acc + load(par + k + off, half) * w_ref[k:k + 1, lanes]
        store(par, half, acc if post is None else post(par, half, acc))
    return acc


def _stage_rmsnorm_bf16(src_ref, g_ref, dst_ref, rows):
    def body(r):
        rs = pl.ds(r, BF16_ROWS)
        dst_ref[rs, :] = _rmsnorm(src_ref[rs, :], g_ref[...]).astype(BF16)
    _row_blocks(rows, body)


def _stage_layernorm_silu_bf16(c_buf, g_ref, b_ref, dst_ref, rows):
    def body(r):
        rs = pl.ds(r, BF16_ROWS)
        dst_ref[rs, :] = _layernorm_silu(_from_slabs(c_buf, rs), g_ref[...], b_ref[...]).astype(BF16)
    _row_blocks(rows, body)


def _mid_rows(p_ref, rs, pair, offset=0):
    cols = [_mid_abs_col(j, pair) - MID0 + offset for j in range(N_SLABS)]
    return jnp.concatenate([p_ref[rs, c:c + LANES] for c in cols], axis=-1)


def _stage_gate_b_bf16(p_ref, c_buf, dst_ref, rows):
    def body(r):
        rs = pl.ds(r, BF16_ROWS)
        dst_ref[rs, :] = (_mid_rows(p_ref, rs, 0) * _from_slabs(c_buf, rs)).astype(BF16)
    _row_blocks(rows, body)


def _stage_merge_bf16(p_ref, ya_ref, yb_ref, dst_ref, rows):
    def body(r):
        rs = pl.ds(r, BF16_ROWS)
        m = (_sigmoid(_mid_rows(p_ref, rs, 2)) * ya_ref[rs, :]
             + _sigmoid(_mid_rows(p_ref, rs, 2, LANES)) * yb_ref[rs, :])
        dst_ref[rs, :] = m.astype(BF16)
    _row_blocks(rows, body)


def _mixer_kernel(x_ref, ha0_ref, hb0_ref, n1g_ref, caw_ref, cab_ref, lng_ref, lnb_ref, cbw_ref,
                  w_in_ref, w_a_ref, w_b_ref, w_o_ref,
                  h_ref, na_ref, nb_ref,
                  ua_buf, zb_buf, bb_buf, c_buf, xn_buf, lhs_a, lhs_b, sga_buf, sgb_buf, yb_buf):
    t = pl.program_id(1)
    last_t = pl.num_programs(1) - 1

    @pl.when(t == 0)
    def _():
        for j in range(N_SLABS):
            ua_buf[j, 0:HA, :] = ha0_ref[:, _lane_slab(j)]
            zb_buf[j, 0:HB, :] = hb0_ref[:, _lane_slab(j)]

    for r in range(0, TM, BF16_ROWS):
        xn_buf[r:r + BF16_ROWS, :] = _rmsnorm(x_ref[r:r + BF16_ROWS, :], n1g_ref[...]).astype(BF16)

    for j in range(N_SLABS):
        r = _dot(xn_buf[...], w_in_ref[:, _glu_col(j):_glu_col(j) + MXU_COLS])
        ua_buf[j, HA:HA + TM, :] = r[:, 0:LANES] * _sigmoid(r[:, LANES:MXU_COLS])

    def conv_a_step(i, carry):
        def proj(col):
            start = pl.multiple_of(MID0 + i * MID_CHUNK + col, MXU_COLS)
            return _dot(xn_buf[...], w_in_ref[:, pl.ds(start, MXU_COLS)])
        r = proj(0)
        for jj in range(SLABS_PER_CHUNK):
            bb_buf[i * SLABS_PER_CHUNK + jj, :, :] = r[:, _lane_slab(jj)]
        for jj in range(SLABS_PER_CHUNK):
            r = proj(_mid_col(jj, 1))
            zb_buf[i * SLABS_PER_CHUNK + jj, HB:HB + TM, :] = r[:, 0:LANES] * r[:, LANES:MXU_COLS]
        for jj in range(SLABS_PER_CHUNK):
            r = _sigmoid(proj(_mid_col(jj, 2)))
            lanes = pl.ds(pl.multiple_of((i * SLABS_PER_CHUNK + jj) * LANES, LANES), LANES)
            sga_buf[:, lanes] = r[:, 0:LANES]
            sgb_buf[:, lanes] = r[:, LANES:MXU_COLS]
        acc = None
        for jj in range(SLABS_PER_CHUNK):
            j = i * SLABS_PER_CHUNK + jj
            lanes = pl.ds(pl.multiple_of(j * LANES, LANES), LANES)
            for r0 in range(0, TM, CONV_ROWS):
                def load(start, n, r0=r0, j=j):
                    return ua_buf[j, pl.ds(r0 + start, n, stride=2), :]
                def store(start, n, v, r0=r0, j=j):
                    c_buf[j, pl.ds(r0 + start, n, stride=2), :] = v
                acc = _conv_taps(load, store, caw_ref, lanes, KA, OFF_A, CONV_ROWS,
                                 lambda n, lanes=lanes: jnp.broadcast_to(cab_ref[:, lanes], (n, LANES)),
                                 after=acc)
        return carry
    lax.fori_loop(0, N_CHUNKS, conv_a_step, 0)

    @pl.when(t == last_t)
    def _():
        for j in range(N_SLABS):
            na_ref[:, _lane_slab(j)] = ua_buf[j, TM + OFF_A:TM + HA, :]
    ua_buf[:, 0:HA, :] = ua_buf[:, TM:TM + HA, :]

    def conv_b(j, lanes):
        for r0 in range(0, TM, CONV_ROWS):
            def load(start, n, r0=r0):
                return zb_buf[j, pl.ds(r0 + start, n, stride=2), :]
            def gate(par, n, acc, r0=r0):
                return acc * bb_buf[j, pl.ds(r0 + par, n, stride=2), :]
            def store(start, n, v, r0=r0):
                bb_buf[j, pl.ds(r0 + start, n, stride=2), :] = v
            _conv_taps(load, store, cbw_ref, lanes, KB, OFF_B, CONV_ROWS,
                       lambda n: jnp.zeros((n, LANES), F32), post=gate)
    _slab_loop(conv_b)

    @pl.when(t == last_t)
    def _():
        for j in range(N_SLABS):
            nb_ref[:, _lane_slab(j)] = zb_buf[j, TM + OFF_B:TM + HB, :]
    zb_buf[:, 0:HB, :] = zb_buf[:, TM:TM + HB, :]

    def pack_b(r):
        rs = pl.ds(r, BF16_ROWS)
        lhs_b[rs, :] = _from_slabs(bb_buf, rs).astype(BF16)
    _row_blocks(TM, pack_b)

    rows_per_step = TM // N_CHUNKS
    def ln_step(i, carry):
        cols = pl.ds(pl.multiple_of(i * MXU_COLS, MXU_COLS), MXU_COLS)
        yb_buf[:, cols] = _dot(lhs_b[...], w_b_ref[:, cols])
        for rr in range(0, rows_per_step, BF16_ROWS):
            rs = pl.ds(pl.multiple_of(i * rows_per_step + rr, BF16_ROWS), BF16_ROWS)
            lhs_a[rs, :] = _layernorm_silu(_from_slabs(c_buf, rs), lng_ref[...], lnb_ref[...]).astype(BF16)
        return carry
    lax.fori_loop(0, N_CHUNKS, ln_step, 0)

    for c in range(N_CHUNKS):
        cols = slice(c * MXU_COLS, (c + 1) * MXU_COLS)
        y_a = _dot(lhs_a[...], w_a_ref[:, cols])
        lhs_b[:, cols] = (sga_buf[:, cols] * y_a + sgb_buf[:, cols] * yb_buf[:, cols]).astype(BF16)

    h_ref[...] = x_ref[...] + _dot(lhs_b[...], w_o_ref[...])


def _vmem_spec():
    return pl.BlockSpec(memory_space=pltpu.VMEM)


def _prompt_mixer(x, ha0, hb0, lw):
    b, s, _ = x.shape
    row_spec = pl.BlockSpec((None, TM, D), lambda i, j: (i, j, 0))
    return pl.pallas_call(
        _mixer_kernel,
        grid=(b, s // TM),
        in_specs=[row_spec] + [_vmem_spec()] * 12,
        out_specs=[row_spec,
                   pl.BlockSpec((None, KA - 1, D), lambda i, j: (i, 0, 0)),
                   pl.BlockSpec((None, KB - 1, D), lambda i, j: (i, 0, 0))],
        out_shape=[jax.ShapeDtypeStruct((b, s, D), F32),
                   jax.ShapeDtypeStruct((b, KA - 1, D), F32),
                   jax.ShapeDtypeStruct((b, KB - 1, D), F32)],
        scratch_shapes=[pltpu.VMEM((N_SLABS, HA + TM, LANES), F32),
                        pltpu.VMEM((N_SLABS, HB + TM, LANES), F32),
                        pltpu.VMEM((N_SLABS, TM, LANES), F32),
                        pltpu.VMEM((N_SLABS, TM, LANES), F32),
                        pltpu.VMEM((TM, D), BF16),
                        pltpu.VMEM((TM, D), BF16),
                        pltpu.VMEM((TM, D), BF16),
                        pltpu.VMEM((TM, D), F32),
                        pltpu.VMEM((TM, D), F32),
                        pltpu.VMEM((TM, D), F32)],
        compiler_params=pltpu.CompilerParams(
            dimension_semantics=("arbitrary", "arbitrary"), vmem_limit_bytes=VMEM_LIMIT),
        name="prompt_mixer",
    )(x, ha0, hb0, lw["norm1_g"], lw["conv_a_w"], lw["conv_a_b"], lw["ln_a_g"], lw["ln_a_b"],
      lw["conv_b_w"], lw["w_in"], lw["w_a_out"], lw["w_b_out"], lw["w_o"])


def _ffn_kernel(final_norm, h_ref, n2g_ref, wg_ref, wu_ref, wd_ref, fg_ref, out_ref, hn_buf, f_buf):
    rows = h_ref.shape[0]
    for r in range(0, rows, BF16_ROWS):
        hn_buf[r:r + BF16_ROWS, :] = _rmsnorm(h_ref[r:r + BF16_ROWS, :], n2g_ref[...]).astype(BF16)
    for c in range(0, D_HID, MXU_COLS):
        g = _dot(hn_buf[...], wg_ref[:, c:c + MXU_COLS])
        u = _dot(hn_buf[...], wu_ref[:, c:c + MXU_COLS])
        f_buf[:, c:c + MXU_COLS] = (_silu(g) * u).astype(BF16)
    out_ref[...] = h_ref[...] + _dot(f_buf[...], wd_ref[...])
    if final_norm:
        for r in range(0, rows, BF16_ROWS):
            out_ref[r:r + BF16_ROWS, :] = _rmsnorm(out_ref[r:r + BF16_ROWS, :], fg_ref[...])


def _ffn_call(h, lw, final_g, final_norm, tm, name):
    n = h.shape[0]
    row_spec = pl.BlockSpec((tm, D), lambda i: (i, 0))
    return pl.pallas_call(
        functools.partial(_ffn_kernel, final_norm),
        grid=(n // tm,),
        in_specs=[row_spec] + [_vmem_spec()] * 5,
        out_specs=row_spec,
        out_shape=jax.ShapeDtypeStruct((n, D), F32),
        scratch_shapes=[pltpu.VMEM((tm, D), BF16),
                        pltpu.VMEM((tm, D_HID), BF16)],
        compiler_params=pltpu.CompilerParams(
            dimension_semantics=("arbitrary",), vmem_limit_bytes=VMEM_LIMIT),
        name=name,
    )(h, lw["norm2_g"], lw["w_ffn_gate"], lw["w_ffn_up"], lw["w_ffn_down"], final_g)


def _small_kernel(n_seq, seq_len,
                  x_ref, ha_ref, hb_ref, n1g_ref, caw_ref, cab_ref, lng_ref, lnb_ref, cbw_ref,
                  w_in_ref, w_a_ref, w_b_ref, w_o_ref,
                  h_ref, na_ref, nb_ref,
                  ua_buf, zb_buf, c_buf, xn_buf, lhs_buf, p_buf, ya_buf, yb_buf):
    rows = n_seq * seq_len
    for j in range(N_SLABS):
        ua_buf[:, j, 0:HA, :] = ha_ref[:, :, _lane_slab(j)]
        zb_buf[:, j, 0:HB, :] = hb_ref[:, :, _lane_slab(j)]

    _stage_rmsnorm_bf16(x_ref, n1g_ref, xn_buf, rows)

    p_buf[:, 0:MID0] = _dot(xn_buf[...], w_in_ref[:, 0:MID0])
    for s in range(n_seq):
        rs = slice(s * seq_len, (s + 1) * seq_len)
        for j in range(N_SLABS):
            c = _glu_col(j)
            ua_buf[s, j, HA:HA + seq_len, :] = p_buf[rs, c:c + LANES] * _sigmoid(p_buf[rs, c + LANES:c + MXU_COLS])

    def conv_a(j, lanes):
        for s in range(n_seq):
            def load(start, n, s=s):
                return ua_buf[s, j, pl.ds(start, n, stride=2), :]
            def store(start, n, v, s=s):
                c_buf[j, pl.ds(s * seq_len + start, n, stride=2), :] = v
            _conv_taps(load, store, caw_ref, lanes, KA, OFF_A, seq_len,
                       lambda n: jnp.broadcast_to(cab_ref[:, lanes], (n, LANES)))
    _slab_loop(conv_a)
    for j in range(N_SLABS):
        na_ref[:, :, _lane_slab(j)] = ua_buf[:, j, seq_len + OFF_A:seq_len + HA, :]

    _stage_layernorm_silu_bf16(c_buf, lng_ref, lnb_ref, lhs_buf, rows)
    ya_buf[...] = _dot(lhs_buf[...], w_a_ref[...])

    p_buf[...] = _dot(xn_buf[...], w_in_ref[:, MID0:])
    for s in range(n_seq):
        rs = slice(s * seq_len, (s + 1) * seq_len)
        for j in range(N_SLABS):
            c = _mid_abs_col(j, 1) - MID0
            zb_buf[s, j, HB:HB + seq_len, :] = p_buf[rs, c:c + LANES] * p_buf[rs, c + LANES:c + MXU_COLS]

    def conv_b(j, lanes):
        for s in range(n_seq):
            def load(start, n, s=s):
                return zb_buf[s, j, pl.ds(start, n, stride=2), :]
            def store(start, n, v, s=s):
                c_buf[j, pl.ds(s * seq_len + start, n, stride=2), :] = v
            _conv_taps(load, store, cbw_ref, lanes, KB, OFF_B, seq_len,
                       lambda n: jnp.zeros((n, LANES), F32))
    _slab_loop(conv_b)
    for j in range(N_SLABS):
        nb_ref[:, :, _lane_slab(j)] = zb_buf[:, j, seq_len + OFF_B:seq_len + HB, :]

    _stage_gate_b_bf16(p_buf, c_buf, lhs_buf, rows)
    yb_buf[...] = _dot(lhs_buf[...], w_b_ref[...])

    _stage_merge_bf16(p_buf, ya_buf, yb_buf, lhs_buf, rows)
    h_ref[...] = x_ref[...] + _dot(lhs_buf[...], w_o_ref[...])


def _small_mixer(x, ha, hb, lw):
    n_seq = ha.shape[0]
    rows = x.shape[0]
    seq_len = rows // n_seq
    return pl.pallas_call(
        functools.partial(_small_kernel, n_seq, seq_len),
        in_specs=[_vmem_spec()] * 13,
        out_specs=[_vmem_spec()] * 3,
        out_shape=[jax.ShapeDtypeStruct((rows, D), F32),
                   jax.ShapeDtypeStruct((n_seq, KA - 1, D), F32),
                   jax.ShapeDtypeStruct((n_seq, KB - 1, D), F32)],
        scratch_shapes=[pltpu.VMEM((n_seq, N_SLABS, HA + seq_len, LANES), F32),
                        pltpu.VMEM((n_seq, N_SLABS, HB + seq_len, LANES), F32),
                        pltpu.VMEM((N_SLABS, rows, LANES), F32),
                        pltpu.VMEM((rows, D), BF16),
                        pltpu.VMEM((rows, D), BF16),
                        pltpu.VMEM((rows, N_GROUPS * D - MID0), F32),
                        pltpu.VMEM((rows, D), F32),
                        pltpu.VMEM((rows, D), F32)],
        compiler_params=pltpu.CompilerParams(vmem_limit_bytes=VMEM_LIMIT),
        name="small_mixer",
    )(x, ha, hb, lw["norm1_g"], lw["conv_a_w"], lw["conv_a_b"], lw["ln_a_g"], lw["ln_a_b"],
      lw["conv_b_w"], lw["w_in"], lw["w_a_out"], lw["w_b_out"], lw["w_o"])


def _front_pad(a, n_rows):
    pad = [(0, 0)] * a.ndim
    pad[-2] = (n_rows - a.shape[-2], 0)
    return jnp.pad(a, pad)


def kernel(x_prompt, x_sample, state_conv_a, state_conv_b, meta_tokens, norm1_g, w_in, conv_a_w, conv_a_b, ln_a_g, ln_a_b, w_a_out, conv_b_w, w_b_out, w_o, norm2_g, w_ffn_gate, w_ffn_up, w_ffn_down, final_norm_g):
    depth = w_in.shape[0]
    b_p, seq, _ = x_prompt.shape
    b_s, seq_s, _ = x_sample.shape
    assert seq % TM == 0 and seq_s == N_META and seq_s % BF16_ROWS == 0

    row = lambda v: v.reshape(1, -1).astype(F32)
    final_g = row(final_norm_g)
    order = _projection_block_order()

    def reorder_projection(w):
        blocks = w.astype(BF16).reshape(D, N_GROUPS * N_SLABS, LANES)
        return jnp.concatenate([blocks[:, k, :] for k in order], axis=-1)

    layers = []
    for l in range(depth):
        layers.append(dict(
            norm1_g=row(norm1_g[l]), conv_a_w=conv_a_w[l], conv_a_b=row(conv_a_b[l]),
            ln_a_g=row(ln_a_g[l]), ln_a_b=row(ln_a_b[l]), conv_b_w=conv_b_w[l],
            w_in=reorder_projection(w_in[l]), w_a_out=w_a_out[l].astype(BF16),
            w_b_out=w_b_out[l].astype(BF16), w_o=w_o[l].astype(BF16),
            norm2_g=row(norm2_g[l]), w_ffn_gate=w_ffn_gate[l].astype(BF16),
            w_ffn_up=w_ffn_up[l].astype(BF16), w_ffn_down=w_ffn_down[l].astype(BF16)))

    xs = jnp.concatenate([meta_tokens.astype(F32), x_sample.reshape(b_s * seq_s, D)], axis=0)
    xp = x_prompt
    pa, pb, sa, sb = [], [], [], []
    for l in range(depth):
        last = l == depth - 1
        lw = layers[l]
        ha = _front_pad(jnp.concatenate([jnp.zeros((1, KA - 1, D), F32), state_conv_a[l]], axis=0), HA)
        hb = _front_pad(jnp.concatenate([jnp.zeros((1, KB - 1, D), F32), state_conv_b[l]], axis=0), HB)
        hs, na_s, nb_s = _small_mixer(xs, ha, hb, lw)
        xs = _ffn_call(hs, lw, final_g, last, hs.shape[0], "small_ffn")
        sa.append(na_s[1:])
        sb.append(nb_s[1:])
        h, na_p, nb_p = _prompt_mixer(xp, _front_pad(na_s[0], HA), _front_pad(nb_s[0], HB), lw)
        pa.append(na_p)
        pb.append(nb_p)
        xp = _ffn_call(h.reshape(b_p * seq, D), lw, final_g, last, TM, "prompt_ffn").reshape(b_p, seq, D)

    y_prompt = xp
    y_sample = xs[N_META:].reshape(b_s, seq_s, D)
    return (y_prompt, y_sample, jnp.stack(pa, axis=0), jnp.stack(pb, axis=0),
            jnp.stack(sa, axis=0), jnp.stack(sb, axis=0))
```

```python
import functools

import jax
import jax.numpy as jnp
from jax import lax
from jax.experimental import pallas as pl
from jax.experimental.pallas import tpu as pltpu

F32 = jnp.float32
BF16 = jnp.bfloat16

D = 1024
D_HID = 2816
KA = 31
KB = 3
N_META = 16
RMS_EPS = 1e-6
LN_EPS = 1e-5

LANES = 128
BF16_ROWS = 16
N_SLABS = D // LANES
MXU_COLS = 256
SLABS_PER_CHUNK = MXU_COLS // LANES
N_CHUNKS = D // MXU_COLS

HA = 32
HB = 8
OFF_A = HA - (KA - 1)
OFF_B = HB - (KB - 1)
TM = 512
CONV_ROWS = 64
CONV_B_ROWS = 128
ROW_UNROLL = 4
VMEM_LIMIT = 60000 * 1024

COL_A_VAL, COL_A_GATE, COL_B_B, COL_B_C, COL_B_X, COL_G_A, COL_G_B = (i * D for i in range(7))
N_IN = 7 * D


def _sigmoid(x):
    return 0.5 * jnp.tanh(0.5 * x) + 0.5


def _silu(x):
    h = 0.5 * x
    return h * jnp.tanh(h) + h


def _rmsnorm(x, g):
    return x * lax.rsqrt(jnp.mean(x * x, axis=-1, keepdims=True) + RMS_EPS) * g


def _layernorm_silu(c, g, b):
    mu = jnp.mean(c, axis=-1, keepdims=True)
    d = c - mu
    var = jnp.mean(d * d, axis=-1, keepdims=True)
    return _silu(d * lax.rsqrt(var + LN_EPS) * g + b)


def _row_blocks(n_rows, body):
    def step(i, carry):
        body(pl.multiple_of(i * BF16_ROWS, BF16_ROWS))
        return carry
    lax.fori_loop(0, n_rows // BF16_ROWS, step, 0, unroll=ROW_UNROLL)


def _slab_loop(body):
    def step(j, carry):
        body(j, pl.ds(pl.multiple_of(j * LANES, LANES), LANES))
        return carry
    lax.fori_loop(0, N_SLABS, step, 0)


def _lane_slab(j):
    return slice(j * LANES, (j + 1) * LANES)


def _dot(a, b):
    return jnp.dot(a, b, preferred_element_type=F32)


def _from_slabs(buf, rs):
    return jnp.concatenate([buf[j, rs, :] for j in range(N_SLABS)], axis=-1)


def _zero_after(v):
    bits = pltpu.bitcast(v, jnp.uint32)
    bits = lax.shift_right_logical(lax.shift_right_logical(bits, jnp.uint32(16)), jnp.uint32(16))
    return pltpu.bitcast(bits, F32)


def _conv_taps(load, store, w_ref, lanes, n_taps, off, rows, init, post=None, after=None):
    half = rows // 2
    acc = after
    for par in range(2):
        acc = init(half) if acc is None else init(half) + _zero_after(acc)
        for k in range(n_taps):
            acc = acc + load(par + k + off, half) * w_ref[k:k + 1, lanes]
        store(par, half, acc if post is None else post(par, half, acc))
    return acc


def _stage_rmsnorm_bf16(src_ref, g_ref, dst_ref, rows):
    def body(r):
        rs = pl.ds(r, BF16_ROWS)
        dst_ref[rs, :] = _rmsnorm(src_ref[rs, :], g_ref[...]).astype(BF16)
    _row_blocks(rows, body)


def _stage_layernorm_silu_bf16(c_buf, g_ref, b_ref, dst_ref, rows):
    def body(r):
        rs = pl.ds(r, BF16_ROWS)
        dst_ref[rs, :] = _layernorm_silu(_from_slabs(c_buf, rs), g_ref[...], b_ref[...]).astype(BF16)
    _row_blocks(rows, body)


def _stage_gate_b_bf16(p_ref, c_buf, dst_ref, rows):
    def body(r):
        rs = pl.ds(r, BF16_ROWS)
        dst_ref[rs, :] = (p_ref[rs, 0:D] * _from_slabs(c_buf, rs)).astype(BF16)
    _row_blocks(rows, body)


def _stage_merge_bf16(g_ref, ya_ref, yb_ref, dst_ref, rows):
    def body(r):
        rs = pl.ds(r, BF16_ROWS)
        m = _sigmoid(g_ref[rs, 0:D]) * ya_ref[rs, :] + _sigmoid(g_ref[rs, D:2 * D]) * yb_ref[rs, :]
        dst_ref[rs, :] = m.astype(BF16)
    _row_blocks(rows, body)


def _mixer_kernel(x_ref, ha0_ref, hb0_ref, n1g_ref, caw_ref, cab_ref, lng_ref, lnb_ref, cbw_ref,
                  w_in_ref, w_a_ref, w_b_ref, w_o_ref,
                  h_ref, na_ref, nb_ref,
                  ua_buf, zb_buf, bb_buf, c_buf, xn_buf, lhs_a, lhs_b, sga_buf, sgb_buf, yb_buf):
    t = pl.program_id(1)
    last_t = pl.num_programs(1) - 1

    @pl.when(t == 0)
    def _():
        for j in range(N_SLABS):
            ua_buf[j, 0:HA, :] = ha0_ref[:, _lane_slab(j)]
            zb_buf[j, 0:HB, :] = hb0_ref[:, _lane_slab(j)]

    for r in range(0, TM, BF16_ROWS):
        xn_buf[r:r + BF16_ROWS, :] = _rmsnorm(x_ref[r:r + BF16_ROWS, :], n1g_ref[...]).astype(BF16)

    for c in range(N_CHUNKS):
        a_val = _dot(xn_buf[...], w_in_ref[:, COL_A_VAL + c * MXU_COLS:COL_A_VAL + (c + 1) * MXU_COLS])
        a_gate = _dot(xn_buf[...], w_in_ref[:, COL_A_GATE + c * MXU_COLS:COL_A_GATE + (c + 1) * MXU_COLS])
        u = a_val * _sigmoid(a_gate)
        for jj in range(SLABS_PER_CHUNK):
            ua_buf[c * SLABS_PER_CHUNK + jj, HA:HA + TM, :] = u[:, _lane_slab(jj)]

    def proj(col0, i):
        return _dot(xn_buf[...], w_in_ref[:, pl.ds(pl.multiple_of(col0 + i * MXU_COLS, MXU_COLS), MXU_COLS)])

    def conv_a_step(i, carry):
        cols = pl.ds(pl.multiple_of(i * MXU_COLS, MXU_COLS), MXU_COLS)
        z = proj(COL_B_C, i) * proj(COL_B_X, i)
        for jj in range(SLABS_PER_CHUNK):
            zb_buf[i * SLABS_PER_CHUNK + jj, HB:HB + TM, :] = z[:, _lane_slab(jj)]
        sga_buf[:, cols] = _sigmoid(proj(COL_G_A, i))
        sgb_buf[:, cols] = _sigmoid(proj(COL_G_B, i))
        acc = None
        for jj in range(SLABS_PER_CHUNK):
            j = i * SLABS_PER_CHUNK + jj
            lanes = pl.ds(pl.multiple_of(j * LANES, LANES), LANES)
            for r0 in range(0, TM, CONV_ROWS):
                def load(start, n, r0=r0, j=j):
                    return ua_buf[j, pl.ds(r0 + start, n, stride=2), :]
                def store(start, n, v, r0=r0, j=j):
                    c_buf[j, pl.ds(r0 + start, n, stride=2), :] = v
                acc = _conv_taps(load, store, caw_ref, lanes, KA, OFF_A, CONV_ROWS,
                                 lambda n, lanes=lanes: jnp.broadcast_to(cab_ref[:, lanes], (n, LANES)),
                                 after=acc)
        return carry
    lax.fori_loop(0, N_CHUNKS, conv_a_step, 0)

    @pl.when(t == last_t)
    def _():
        for j in range(N_SLABS):
            na_ref[:, _lane_slab(j)] = ua_buf[j, TM + OFF_A:TM + HA, :]
    ua_buf[:, 0:HA, :] = ua_buf[:, TM:TM + HA, :]

    def conv_b_step(i, carry):
        b_b = proj(COL_B_B, i)
        for jj in range(SLABS_PER_CHUNK):
            bb_buf[i * SLABS_PER_CHUNK + jj, :, :] = b_b[:, _lane_slab(jj)]
        for jj in range(SLABS_PER_CHUNK):
            j = i * SLABS_PER_CHUNK + jj
            lanes = pl.ds(pl.multiple_of(j * LANES, LANES), LANES)
            for r0 in range(0, TM, CONV_B_ROWS):
                def load(start, n, r0=r0, j=j):
                    return zb_buf[j, pl.ds(r0 + start, n, stride=2), :]
                def gate(par, n, acc, r0=r0, j=j):
                    return acc * bb_buf[j, pl.ds(r0 + par, n, stride=2), :]
                def store(start, n, v, r0=r0, j=j):
                    bb_buf[j, pl.ds(r0 + start, n, stride=2), :] = v
                _conv_taps(load, store, cbw_ref, lanes, KB, OFF_B, CONV_B_ROWS,
                           lambda n: jnp.zeros((n, LANES), F32), post=gate)
        return carry
    lax.fori_loop(0, N_CHUNKS, conv_b_step, 0)

    @pl.when(t == last_t)
    def _():
        for j in range(N_SLABS):
            nb_ref[:, _lane_slab(j)] = zb_buf[j, TM + OFF_B:TM + HB, :]
    zb_buf[:, 0:HB, :] = zb_buf[:, TM:TM + HB, :]

    def pack_b(r):
        rs = pl.ds(r, BF16_ROWS)
        lhs_b[rs, :] = _from_slabs(bb_buf, rs).astype(BF16)
    _row_blocks(TM, pack_b)

    rows_per_step = TM // N_CHUNKS
    def ln_step(i, carry):
        cols = pl.ds(pl.multiple_of(i * MXU_COLS, MXU_COLS), MXU_COLS)
        yb_buf[:, cols] = _dot(lhs_b[...], w_b_ref[:, cols])
        for rr in range(0, rows_per_step, BF16_ROWS):
            rs = pl.ds(pl.multiple_of(i * rows_per_step + rr, BF16_ROWS), BF16_ROWS)
            lhs_a[rs, :] = _layernorm_silu(_from_slabs(c_buf, rs), lng_ref[...], lnb_ref[...]).astype(BF16)
        return carry
    lax.fori_loop(0, N_CHUNKS, ln_step, 0)

    for c in range(N_CHUNKS):
        cols = slice(c * MXU_COLS, (c + 1) * MXU_COLS)
        y_a = _dot(lhs_a[...], w_a_ref[:, cols])
        lhs_b[:, cols] = (sga_buf[:, cols] * y_a + sgb_buf[:, cols] * yb_buf[:, cols]).astype(BF16)

    h_ref[...] = x_ref[...] + _dot(lhs_b[...], w_o_ref[...])


def _vmem_spec():
    return pl.BlockSpec(memory_space=pltpu.VMEM)


def _layer_spec(stacked, layer):
    zeros = (0,) * (stacked.ndim - 1)
    return pl.BlockSpec((None,) + stacked.shape[1:], lambda *_: (layer,) + zeros,
                        pipeline_mode=pl.Buffered(1))


MIXER_PARAMS = ("norm1_g", "conv_a_w", "conv_a_b", "ln_a_g", "ln_a_b", "conv_b_w",
                "w_in", "w_a_out", "w_b_out", "w_o")
FFN_PARAMS = ("norm2_g", "w_ffn_gate", "w_ffn_up", "w_ffn_down")


def _prompt_mixer(x, ha0, hb0, params, layer):
    b, s, _ = x.shape
    row_spec = pl.BlockSpec((None, TM, D), lambda i, j: (i, j, 0))
    weights = [params[k] for k in MIXER_PARAMS]
    return pl.pallas_call(
        _mixer_kernel,
        grid=(b, s // TM),
        in_specs=[row_spec, _vmem_spec(), _vmem_spec()] + [_layer_spec(w, layer) for w in weights],
        out_specs=[row_spec,
                   pl.BlockSpec((None, KA - 1, D), lambda i, j: (i, 0, 0)),
                   pl.BlockSpec((None, KB - 1, D), lambda i, j: (i, 0, 0))],
        out_shape=[jax.ShapeDtypeStruct((b, s, D), F32),
                   jax.ShapeDtypeStruct((b, KA - 1, D), F32),
                   jax.ShapeDtypeStruct((b, KB - 1, D), F32)],
        scratch_shapes=[pltpu.VMEM((N_SLABS, HA + TM, LANES), F32),
                        pltpu.VMEM((N_SLABS, HB + TM, LANES), F32),
                        pltpu.VMEM((N_SLABS, TM, LANES), F32),
                        pltpu.VMEM((N_SLABS, TM, LANES), F32),
                        pltpu.VMEM((TM, D), BF16),
                        pltpu.VMEM((TM, D), BF16),
                        pltpu.VMEM((TM, D), BF16),
                        pltpu.VMEM((TM, D), F32),
                        pltpu.VMEM((TM, D), F32),
                        pltpu.VMEM((TM, D), F32)],
        compiler_params=pltpu.CompilerParams(
            dimension_semantics=("arbitrary", "arbitrary"), vmem_limit_bytes=VMEM_LIMIT),
        name="prompt_mixer",
    )(x, ha0, hb0, *weights)


def _ffn_kernel(final_norm, h_ref, n2g_ref, wg_ref, wu_ref, wd_ref, fg_ref, out_ref, hn_buf, f_buf):
    rows = h_ref.shape[0]
    for r in range(0, rows, BF16_ROWS):
        hn_buf[r:r + BF16_ROWS, :] = _rmsnorm(h_ref[r:r + BF16_ROWS, :], n2g_ref[...]).astype(BF16)
    for c in range(0, D_HID, MXU_COLS):
        g = _dot(hn_buf[...], wg_ref[:, c:c + MXU_COLS])
        u = _dot(hn_buf[...], wu_ref[:, c:c + MXU_COLS])
        f_buf[:, c:c + MXU_COLS] = (_silu(g) * u).astype(BF16)
    out_ref[...] = h_ref[...] + _dot(f_buf[...], wd_ref[...])
    if final_norm:
        for r in range(0, rows, BF16_ROWS):
            out_ref[r:r + BF16_ROWS, :] = _rmsnorm(out_ref[r:r + BF16_ROWS, :], fg_ref[...])


def _ffn_call(h, params, layer, final_norm, tm, name):
    n = h.shape[0]
    row_spec = pl.BlockSpec((tm, D), lambda i: (i, 0))
    weights = [params[k] for k in FFN_PARAMS]
    return pl.pallas_call(
        functools.partial(_ffn_kernel, final_norm),
        grid=(n // tm,),
        in_specs=[row_spec] + [_layer_spec(w, layer) for w in weights] + [_vmem_spec()],
        out_specs=row_spec,
        out_shape=jax.ShapeDtypeStruct((n, D), F32),
        scratch_shapes=[pltpu.VMEM((tm, D), BF16),
                        pltpu.VMEM((tm, D_HID), BF16)],
        compiler_params=pltpu.CompilerParams(
            dimension_semantics=("arbitrary",), vmem_limit_bytes=VMEM_LIMIT),
        name=name,
    )(h, *weights, params["final_norm_g"])


def _small_kernel(n_seq, seq_len,
                  x_ref, ha_ref, hb_ref, n1g_ref, caw_ref, cab_ref, lng_ref, lnb_ref, cbw_ref,
                  w_in_ref, w_a_ref, w_b_ref, w_o_ref,
                  h_ref, na_ref, nb_ref,
                  ua_buf, zb_buf, c_buf, xn_buf, lhs_buf, p_buf, ya_buf, yb_buf):
    rows = n_seq * seq_len
    for j in range(N_SLABS):
        ua_buf[:, j, 0:HA, :] = ha_ref[:, :, _lane_slab(j)]
        zb_buf[:, j, 0:HB, :] = hb_ref[:, :, _lane_slab(j)]

    _stage_rmsnorm_bf16(x_ref, n1g_ref, xn_buf, rows)

    p_buf[:, 0:2 * D] = _dot(xn_buf[...], w_in_ref[:, COL_A_VAL:COL_B_B])
    for s in range(n_seq):
        rs = slice(s * seq_len, (s + 1) * seq_len)
        u = p_buf[rs, 0:D] * _sigmoid(p_buf[rs, D:2 * D])
        for j in range(N_SLABS):
            ua_buf[s, j, HA:HA + seq_len, :] = u[:, _lane_slab(j)]

    def conv_a(j, lanes):
        for s in range(n_seq):
            def load(start, n, s=s):
                return ua_buf[s, j, pl.ds(start, n, stride=2), :]
            def store(start, n, v, s=s):
                c_buf[j, pl.ds(s * seq_len + start, n, stride=2), :] = v
            _conv_taps(load, store, caw_ref, lanes, KA, OFF_A, seq_len,
                       lambda n: jnp.broadcast_to(cab_ref[:, lanes], (n, LANES)))
    _slab_loop(conv_a)
    for j in range(N_SLABS):
        na_ref[:, :, _lane_slab(j)] = ua_buf[:, j, seq_len + OFF_A:seq_len + HA, :]

    _stage_layernorm_silu_bf16(c_buf, lng_ref, lnb_ref, lhs_buf, rows)
    ya_buf[...] = _dot(lhs_buf[...], w_a_ref[...])

    p_buf[...] = _dot(xn_buf[...], w_in_ref[:, COL_B_B:COL_G_A])
    for s in range(n_seq):
        rs = slice(s * seq_len, (s + 1) * seq_len)
        z = p_buf[rs, D:2 * D] * p_buf[rs, 2 * D:3 * D]
        for j in range(N_SLABS):
            zb_buf[s, j, HB:HB + seq_len, :] = z[:, _lane_slab(j)]

    def conv_b(j, lanes):
        for s in range(n_seq):
            def load(start, n, s=s):
                return zb_buf[s, j, pl.ds(start, n, stride=2), :]
            def store(start, n, v, s=s):
                c_buf[j, pl.ds(s * seq_len + start, n, stride=2), :] = v
            _conv_taps(load, store, cbw_ref, lanes, KB, OFF_B, seq_len,
                       lambda n: jnp.zeros((n, LANES), F32))
    _slab_loop(conv_b)
    for j in range(N_SLABS):
        nb_ref[:, :, _lane_slab(j)] = zb_buf[:, j, seq_len + OFF_B:seq_len + HB, :]

    _stage_gate_b_bf16(p_buf, c_buf, lhs_buf, rows)
    yb_buf[...] = _dot(lhs_buf[...], w_b_ref[...])

    p_buf[:, 0:2 * D] = _dot(xn_buf[...], w_in_ref[:, COL_G_A:N_IN])
    _stage_merge_bf16(p_buf, ya_buf, yb_buf, lhs_buf, rows)
    h_ref[...] = x_ref[...] + _dot(lhs_buf[...], w_o_ref[...])


def _whole_spec(shape):
    zeros = (0,) * len(shape)
    return pl.BlockSpec(shape, lambda *_: zeros)


def _small_mixer(x, ha, hb, params, layer):
    n_seq = ha.shape[0]
    rows = x.shape[0]
    seq_len = rows // n_seq
    weights = [params[k] for k in MIXER_PARAMS]
    out_shape = [jax.ShapeDtypeStruct((rows, D), F32),
                 jax.ShapeDtypeStruct((n_seq, KA - 1, D), F32),
                 jax.ShapeDtypeStruct((n_seq, KB - 1, D), F32)]
    return pl.pallas_call(
        functools.partial(_small_kernel, n_seq, seq_len),
        grid=(1,),
        in_specs=[_whole_spec(a.shape) for a in (x, ha, hb)] + [_layer_spec(w, layer) for w in weights],
        out_specs=[_whole_spec(o.shape) for o in out_shape],
        out_shape=out_shape,
        scratch_shapes=[pltpu.VMEM((n_seq, N_SLABS, HA + seq_len, LANES), F32),
                        pltpu.VMEM((n_seq, N_SLABS, HB + seq_len, LANES), F32),
                        pltpu.VMEM((N_SLABS, rows, LANES), F32),
                        pltpu.VMEM((rows, D), BF16),
                        pltpu.VMEM((rows, D), BF16),
                        pltpu.VMEM((rows, 3 * D), F32),
                        pltpu.VMEM((rows, D), F32),
                        pltpu.VMEM((rows, D), F32)],
        compiler_params=pltpu.CompilerParams(
            dimension_semantics=("arbitrary",), vmem_limit_bytes=VMEM_LIMIT),
        name="small_mixer",
    )(x, ha, hb, *weights)


def _front_pad(a, n_rows):
    pad = [(0, 0)] * a.ndim
    pad[-2] = (n_rows - a.shape[-2], 0)
    return jnp.pad(a, pad)


def kernel(x_prompt, x_sample, state_conv_a, state_conv_b, meta_tokens, norm1_g, w_in, conv_a_w, conv_a_b, ln_a_g, ln_a_b, w_a_out, conv_b_w, w_b_out, w_o, norm2_g, w_ffn_gate, w_ffn_up, w_ffn_down, final_norm_g):
    depth = w_in.shape[0]
    b_p, seq, _ = x_prompt.shape
    b_s, seq_s, _ = x_sample.shape
    assert seq % TM == 0 and seq_s == N_META and seq_s % BF16_ROWS == 0

    rows_of = lambda v: v.reshape(depth, 1, -1).astype(F32)
    params = dict(
        norm1_g=rows_of(norm1_g), conv_a_w=conv_a_w, conv_a_b=rows_of(conv_a_b),
        ln_a_g=rows_of(ln_a_g), ln_a_b=rows_of(ln_a_b), conv_b_w=conv_b_w,
        w_in=w_in.astype(BF16), w_a_out=w_a_out.astype(BF16), w_b_out=w_b_out.astype(BF16),
        w_o=w_o.astype(BF16), norm2_g=rows_of(norm2_g), w_ffn_gate=w_ffn_gate.astype(BF16),
        w_ffn_up=w_ffn_up.astype(BF16), w_ffn_down=w_ffn_down.astype(BF16),
        final_norm_g=final_norm_g.reshape(1, -1).astype(F32))

    xs = jnp.concatenate([meta_tokens.astype(F32), x_sample.reshape(b_s * seq_s, D)], axis=0)
    xp = x_prompt
    pa, pb, sa, sb = [], [], [], []
    for l in range(depth):
        last = l == depth - 1
        ha = _front_pad(jnp.concatenate([jnp.zeros((1, KA - 1, D), F32), state_conv_a[l]], axis=0), HA)
        hb = _front_pad(jnp.concatenate([jnp.zeros((1, KB - 1, D), F32), state_conv_b[l]], axis=0), HB)
        hs, na_s, nb_s = _small_mixer(xs, ha, hb, params, l)
        xs = _ffn_call(hs, params, l, last, hs.shape[0], "small_ffn")
        sa.append(na_s[1:])
        sb.append(nb_s[1:])
        h, na_p, nb_p = _prompt_mixer(xp, _front_pad(na_s[0], HA), _front_pad(nb_s[0], HB), params, l)
        pa.append(na_p)
        pb.append(nb_p)
        xp = _ffn_call(h.reshape(b_p * seq, D), params, l, last, TM, "prompt_ffn").reshape(b_p, seq, D)

    y_prompt = xp
    y_sample = xs[N_META:].reshape(b_s, seq_s, D)
    return (y_prompt, y_sample, jnp.stack(pa, axis=0), jnp.stack(pb, axis=0),
            jnp.stack(sa, axis=0), jnp.stack(sb, axis=0))
```

```python
import functools

import jax
import jax.numpy as jnp
from jax import lax
from jax.experimental import pallas as pl
from jax.experimental.pallas import tpu as pltpu

F32 = jnp.float32
BF16 = jnp.bfloat16

D = 1024
D_HID = 2816
KA = 31
KB = 3
N_META = 16
RMS_EPS = 1e-6
LN_EPS = 1e-5

LANES = 128
BF16_ROWS = 16
N_SLABS = D // LANES
MXU_COLS = 256
SLABS_PER_CHUNK = MXU_COLS // LANES
N_CHUNKS = D // MXU_COLS

HA = 32
HB = 8
OFF_A = HA - (KA - 1)
OFF_B = HB - (KB - 1)
TM = 512
CONV_ROWS = 64
CONV_B_ROWS = 128
ROW_UNROLL = 4
VMEM_LIMIT = 60000 * 1024

COL_A_VAL, COL_A_GATE, COL_B_B, COL_B_C, COL_B_X, COL_G_A, COL_G_B = (i * D for i in range(7))
N_IN = 7 * D


def _sigmoid(x):
    return 0.5 * jnp.tanh(0.5 * x) + 0.5


def _silu(x):
    h = 0.5 * x
    return h * jnp.tanh(h) + h


def _rmsnorm(x, g):
    return x * lax.rsqrt(jnp.mean(x * x, axis=-1, keepdims=True) + RMS_EPS) * g


def _layernorm_silu(c, g, b):
    mu = jnp.mean(c, axis=-1, keepdims=True)
    d = c - mu
    var = jnp.mean(d * d, axis=-1, keepdims=True)
    return _silu(d * lax.rsqrt(var + LN_EPS) * g + b)


def _row_blocks(n_rows, body):
    def step(i, carry):
        body(pl.multiple_of(i * BF16_ROWS, BF16_ROWS))
        return carry
    lax.fori_loop(0, n_rows // BF16_ROWS, step, 0, unroll=ROW_UNROLL)


def _slab_loop(body):
    def step(j, carry):
        body(j, pl.ds(pl.multiple_of(j * LANES, LANES), LANES))
        return carry
    lax.fori_loop(0, N_SLABS, step, 0)


def _lane_slab(j):
    return slice(j * LANES, (j + 1) * LANES)


def _dot(a, b):
    return jnp.dot(a, b, preferred_element_type=F32)


def _from_slabs(buf, rs):
    return jnp.concatenate([buf[j, rs, :] for j in range(N_SLABS)], axis=-1)


def _zero_after(v):
    bits = pltpu.bitcast(v, jnp.uint32)
    bits = lax.shift_right_logical(lax.shift_right_logical(bits, jnp.uint32(16)), jnp.uint32(16))
    return pltpu.bitcast(bits, F32)


def _conv_taps(load, store, w_ref, lanes, n_taps, off, rows, init, post=None, after=None):
    half = rows // 2
    acc = after
    for par in range(2):
        acc = init(half) if acc is None else init(half) + _zero_after(acc)
        for k in range(n_taps):
            acc = acc + load(par + k + off, half) * w_ref[k:k + 1, lanes]
        store(par, half, acc if post is None else post(par, half, acc))
    return acc


def _stage_rmsnorm_bf16(src_ref, g_ref, dst_ref, rows):
    def body(r):
        rs = pl.ds(r, BF16_ROWS)
        dst_ref[rs, :] = _rmsnorm(src_ref[rs, :], g_ref[...]).astype(BF16)
    _row_blocks(rows, body)


def _stage_layernorm_silu_bf16(c_buf, g_ref, b_ref, dst_ref, rows):
    def body(r):
        rs = pl.ds(r, BF16_ROWS)
        dst_ref[rs, :] = _layernorm_silu(_from_slabs(c_buf, rs), g_ref[...], b_ref[...]).astype(BF16)
    _row_blocks(rows, body)


def _stage_gate_b_bf16(p_ref, c_buf, dst_ref, rows):
    def body(r):
        rs = pl.ds(r, BF16_ROWS)
        dst_ref[rs, :] = (p_ref[rs, 0:D] * _from_slabs(c_buf, rs)).astype(BF16)
    _row_blocks(rows, body)


def _stage_merge_bf16(g_ref, ya_ref, yb_ref, dst_ref, rows):
    def body(r):
        rs = pl.ds(r, BF16_ROWS)
        m = _sigmoid(g_ref[rs, 0:D]) * ya_ref[rs, :] + _sigmoid(g_ref[rs, D:2 * D]) * yb_ref[rs, :]
        dst_ref[rs, :] = m.astype(BF16)
    _row_blocks(rows, body)


def _mixer_kernel(x_ref, ha0_ref, hb0_ref, n1g_ref, caw_ref, cab_ref, lng_ref, lnb_ref, cbw_ref,
                  w_in_ref, w_a_ref, w_b_ref, w_o_ref,
                  h_ref, na_ref, nb_ref,
                  ua_buf, zb_buf, bb_buf, c_buf, xn_buf, lhs_a, lhs_b, lhs_m, ga_buf, gb_buf, yb_buf):
    t = pl.program_id(1)
    last_t = pl.num_programs(1) - 1

    @pl.when(t == 0)
    def _():
        for j in range(N_SLABS):
            ua_buf[j, 0:HA, :] = ha0_ref[:, _lane_slab(j)]
            zb_buf[j, 0:HB, :] = hb0_ref[:, _lane_slab(j)]

    for r in range(0, TM, BF16_ROWS):
        xn_buf[r:r + BF16_ROWS, :] = _rmsnorm(x_ref[r:r + BF16_ROWS, :], n1g_ref[...]).astype(BF16)

    for c in range(N_CHUNKS):
        a_val = _dot(xn_buf[...], w_in_ref[:, COL_A_VAL + c * MXU_COLS:COL_A_VAL + (c + 1) * MXU_COLS])
        a_gate = _dot(xn_buf[...], w_in_ref[:, COL_A_GATE + c * MXU_COLS:COL_A_GATE + (c + 1) * MXU_COLS])
        u = a_val * _sigmoid(a_gate)
        for jj in range(SLABS_PER_CHUNK):
            ua_buf[c * SLABS_PER_CHUNK + jj, HA:HA + TM, :] = u[:, _lane_slab(jj)]

    def proj(col0, i):
        return _dot(xn_buf[...], w_in_ref[:, pl.ds(pl.multiple_of(col0 + i * MXU_COLS, MXU_COLS), MXU_COLS)])

    def conv_a_step(i, carry):
        cols = pl.ds(pl.multiple_of(i * MXU_COLS, MXU_COLS), MXU_COLS)
        z = proj(COL_B_C, i) * proj(COL_B_X, i)
        for jj in range(SLABS_PER_CHUNK):
            zb_buf[i * SLABS_PER_CHUNK + jj, HB:HB + TM, :] = z[:, _lane_slab(jj)]
        ga_buf[:, cols] = proj(COL_G_A, i)
        gb_buf[:, cols] = proj(COL_G_B, i)
        acc = None
        for jj in range(SLABS_PER_CHUNK):
            j = i * SLABS_PER_CHUNK + jj
            lanes = pl.ds(pl.multiple_of(j * LANES, LANES), LANES)
            for r0 in range(0, TM, CONV_ROWS):
                def load(start, n, r0=r0, j=j):
                    return ua_buf[j, pl.ds(r0 + start, n, stride=2), :]
                def store(start, n, v, r0=r0, j=j):
                    c_buf[j, pl.ds(r0 + start, n, stride=2), :] = v
                acc = _conv_taps(load, store, caw_ref, lanes, KA, OFF_A, CONV_ROWS,
                                 lambda n, lanes=lanes: jnp.broadcast_to(cab_ref[:, lanes], (n, LANES)),
                                 after=acc)
        return carry
    lax.fori_loop(0, N_CHUNKS, conv_a_step, 0)


    for i in range(N_CHUNKS):
        b_b = proj(COL_B_B, i)
        for jj in range(SLABS_PER_CHUNK):
            bb_buf[i * SLABS_PER_CHUNK + jj, :, :] = b_b[:, _lane_slab(jj)]
        for jj in range(SLABS_PER_CHUNK):
            j = i * SLABS_PER_CHUNK + jj
            for r0 in range(0, TM, CONV_B_ROWS):
                def load(start, n, r0=r0, j=j):
                    return zb_buf[j, pl.ds(r0 + start, n, stride=2), :]
                def gate(par, n, acc, r0=r0, j=j):
                    return acc * bb_buf[j, pl.ds(r0 + par, n, stride=2), :]
                def store(start, n, v, r0=r0, j=j):
                    bb_buf[j, pl.ds(r0 + start, n, stride=2), :] = v
                _conv_taps(load, store, cbw_ref, _lane_slab(j), KB, OFF_B, CONV_B_ROWS,
                           lambda n: jnp.zeros((n, LANES), F32), post=gate)

    for r in range(0, TM, BF16_ROWS):
        lhs_b[r:r + BF16_ROWS, :] = _from_slabs(bb_buf, slice(r, r + BF16_ROWS)).astype(BF16)

    rows_per_chunk = TM // N_CHUNKS
    for c in range(N_CHUNKS):
        cols = slice(c * MXU_COLS, (c + 1) * MXU_COLS)
        yb_buf[:, cols] = _dot(lhs_b[...], w_b_ref[:, cols])
        for r in range(c * rows_per_chunk, (c + 1) * rows_per_chunk, BF16_ROWS):
            rs = slice(r, r + BF16_ROWS)
            lhs_a[rs, :] = _layernorm_silu(_from_slabs(c_buf, rs), lng_ref[...], lnb_ref[...]).astype(BF16)

    for c in range(N_CHUNKS):
        cols = slice(c * MXU_COLS, (c + 1) * MXU_COLS)
        y_a = _dot(lhs_a[...], w_a_ref[:, cols])
        lhs_m[:, cols] = (_sigmoid(ga_buf[:, cols]) * y_a
                          + _sigmoid(gb_buf[:, cols]) * yb_buf[:, cols]).astype(BF16)

    h_ref[...] = x_ref[...] + _dot(lhs_m[...], w_o_ref[...])

    @pl.when(t == last_t)
    def _():
        for j in range(N_SLABS):
            na_ref[:, _lane_slab(j)] = ua_buf[j, TM + OFF_A:TM + HA, :]
            nb_ref[:, _lane_slab(j)] = zb_buf[j, TM + OFF_B:TM + HB, :]
    ua_buf[:, 0:HA, :] = ua_buf[:, TM:TM + HA, :]
    zb_buf[:, 0:HB, :] = zb_buf[:, TM:TM + HB, :]


def _vmem_spec():
    return pl.BlockSpec(memory_space=pltpu.VMEM)


def _layer_spec(stacked, layer):
    zeros = (0,) * (stacked.ndim - 1)
    return pl.BlockSpec((None,) + stacked.shape[1:], lambda *_: (layer,) + zeros,
                        pipeline_mode=pl.Buffered(1))


MIXER_PARAMS = ("norm1_g", "conv_a_w", "conv_a_b", "ln_a_g", "ln_a_b", "conv_b_w",
                "w_in", "w_a_out", "w_b_out", "w_o")
FFN_PARAMS = ("norm2_g", "w_ffn_gate", "w_ffn_up", "w_ffn_down")


def _prompt_mixer(x, ha0, hb0, params, layer):
    b, s, _ = x.shape
    row_spec = pl.BlockSpec((None, TM, D), lambda i, j: (i, j, 0))
    weights = [params[k] for k in MIXER_PARAMS]
    return pl.pallas_call(
        _mixer_kernel,
        grid=(b, s // TM),
        in_specs=[row_spec, _vmem_spec(), _vmem_spec()] + [_layer_spec(w, layer) for w in weights],
        out_specs=[row_spec,
                   pl.BlockSpec((None, KA - 1, D), lambda i, j: (i, 0, 0)),
                   pl.BlockSpec((None, KB - 1, D), lambda i, j: (i, 0, 0))],
        out_shape=[jax.ShapeDtypeStruct((b, s, D), F32),
                   jax.ShapeDtypeStruct((b, KA - 1, D), F32),
                   jax.ShapeDtypeStruct((b, KB - 1, D), F32)],
        scratch_shapes=[pltpu.VMEM((N_SLABS, HA + TM, LANES), F32),
                        pltpu.VMEM((N_SLABS, HB + TM, LANES), F32),
                        pltpu.VMEM((N_SLABS, TM, LANES), F32),
                        pltpu.VMEM((N_SLABS, TM, LANES), F32),
                        pltpu.VMEM((TM, D), BF16),
                        pltpu.VMEM((TM, D), BF16),
                        pltpu.VMEM((TM, D), BF16),
                        pltpu.VMEM((TM, D), BF16),
                        pltpu.VMEM((TM, D), F32),
                        pltpu.VMEM((TM, D), F32),
                        pltpu.VMEM((TM, D), F32)],
        compiler_params=pltpu.CompilerParams(
            dimension_semantics=("arbitrary", "arbitrary"), vmem_limit_bytes=VMEM_LIMIT),
        name="prompt_mixer",
    )(x, ha0, hb0, *weights)


def _ffn_kernel(final_norm, h_ref, n2g_ref, wg_ref, wu_ref, wd_ref, fg_ref, out_ref, hn_buf, f_buf):
    rows = h_ref.shape[0]
    for r in range(0, rows, BF16_ROWS):
        hn_buf[r:r + BF16_ROWS, :] = _rmsnorm(h_ref[r:r + BF16_ROWS, :], n2g_ref[...]).astype(BF16)
    for c in range(0, D_HID, MXU_COLS):
        g = _dot(hn_buf[...], wg_ref[:, c:c + MXU_COLS])
        u = _dot(hn_buf[...], wu_ref[:, c:c + MXU_COLS])
        f_buf[:, c:c + MXU_COLS] = (_silu(g) * u).astype(BF16)
    out_ref[...] = h_ref[...] + _dot(f_buf[...], wd_ref[...])
    if final_norm:
        for r in range(0, rows, BF16_ROWS):
            out_ref[r:r + BF16_ROWS, :] = _rmsnorm(out_ref[r:r + BF16_ROWS, :], fg_ref[...])


def _ffn_call(h, params, layer, final_norm, tm, name):
    n = h.shape[0]
    row_spec = pl.BlockSpec((tm, D), lambda i: (i, 0))
    weights = [params[k] for k in FFN_PARAMS]
    return pl.pallas_call(
        functools.partial(_ffn_kernel, final_norm),
        grid=(n // tm,),
        in_specs=[row_spec] + [_layer_spec(w, layer) for w in weights] + [_vmem_spec()],
        out_specs=row_spec,
        out_shape=jax.ShapeDtypeStruct((n, D), F32),
        scratch_shapes=[pltpu.VMEM((tm, D), BF16),
                        pltpu.VMEM((tm, D_HID), BF16)],
        compiler_params=pltpu.CompilerParams(
            dimension_semantics=("arbitrary",), vmem_limit_bytes=VMEM_LIMIT),
        name=name,
    )(h, *weights, params["final_norm_g"])


def _small_kernel(n_seq, seq_len,
                  x_ref, ha_ref, hb_ref, n1g_ref, caw_ref, cab_ref, lng_ref, lnb_ref, cbw_ref,
                  w_in_ref, w_a_ref, w_b_ref, w_o_ref,
                  h_ref, na_ref, nb_ref,
                  ua_buf, zb_buf, c_buf, xn_buf, lhs_buf, p_buf, ya_buf, yb_buf):
    rows = n_seq * seq_len
    for j in range(N_SLABS):
        ua_buf[:, j, 0:HA, :] = ha_ref[:, :, _lane_slab(j)]
        zb_buf[:, j, 0:HB, :] = hb_ref[:, :, _lane_slab(j)]

    _stage_rmsnorm_bf16(x_ref, n1g_ref, xn_buf, rows)

    p_buf[:, 0:2 * D] = _dot(xn_buf[...], w_in_ref[:, COL_A_VAL:COL_B_B])
    for s in range(n_seq):
        rs = slice(s * seq_len, (s + 1) * seq_len)
        u = p_buf[rs, 0:D] * _sigmoid(p_buf[rs, D:2 * D])
        for j in range(N_SLABS):
            ua_buf[s, j, HA:HA + seq_len, :] = u[:, _lane_slab(j)]

    def conv_a(j, lanes):
        for s in range(n_seq):
            def load(start, n, s=s):
                return ua_buf[s, j, pl.ds(start, n, stride=2), :]
            def store(start, n, v, s=s):
                c_buf[j, pl.ds(s * seq_len + start, n, stride=2), :] = v
            _conv_taps(load, store, caw_ref, lanes, KA, OFF_A, seq_len,
                       lambda n: jnp.broadcast_to(cab_ref[:, lanes], (n, LANES)))
    _slab_loop(conv_a)
    for j in range(N_SLABS):
        na_ref[:, :, _lane_slab(j)] = ua_buf[:, j, seq_len + OFF_A:seq_len + HA, :]

    _stage_layernorm_silu_bf16(c_buf, lng_ref, lnb_ref, lhs_buf, rows)
    ya_buf[...] = _dot(lhs_buf[...], w_a_ref[...])

    p_buf[...] = _dot(xn_buf[...], w_in_ref[:, COL_B_B:COL_G_A])
    for s in range(n_seq):
        rs = slice(s * seq_len, (s + 1) * seq_len)
        z = p_buf[rs, D:2 * D] * p_buf[rs, 2 * D:3 * D]
        for j in range(N_SLABS):
            zb_buf[s, j, HB:HB + seq_len, :] = z[:, _lane_slab(j)]

    def conv_b(j, lanes):
        for s in range(n_seq):
            def load(start, n, s=s):
                return zb_buf[s, j, pl.ds(start, n, stride=2), :]
            def store(start, n, v, s=s):
                c_buf[j, pl.ds(s * seq_len + start, n, stride=2), :] = v
            _conv_taps(load, store, cbw_ref, lanes, KB, OFF_B, seq_len,
                       lambda n: jnp.zeros((n, LANES), F32))
    _slab_loop(conv_b)
    for j in range(N_SLABS):
        nb_ref[:, :, _lane_slab(j)] = zb_buf[:, j, seq_len + OFF_B:seq_len + HB, :]

    _stage_gate_b_bf16(p_buf, c_buf, lhs_buf, rows)
    yb_buf[...] = _dot(lhs_buf[...], w_b_ref[...])

    p_buf[:, 0:2 * D] = _dot(xn_buf[...], w_in_ref[:, COL_G_A:N_IN])
    _stage_merge_bf16(p_buf, ya_buf, yb_buf, lhs_buf, rows)
    h_ref[...] = x_ref[...] + _dot(lhs_buf[...], w_o_ref[...])


def _whole_spec(shape):
    zeros = (0,) * len(shape)
    return pl.BlockSpec(shape, lambda *_: zeros)


def _small_mixer(x, ha, hb, params, layer):
    n_seq = ha.shape[0]
    rows = x.shape[0]
    seq_len = rows // n_seq
    weights = [params[k] for k in MIXER_PARAMS]
    out_shape = [jax.ShapeDtypeStruct((rows, D), F32),
                 jax.ShapeDtypeStruct((n_seq, KA - 1, D), F32),
                 jax.ShapeDtypeStruct((n_seq, KB - 1, D), F32)]
    return pl.pallas_call(
        functools.partial(_small_kernel, n_seq, seq_len),
        grid=(1,),
        in_specs=[_whole_spec(a.shape) for a in (x, ha, hb)] + [_layer_spec(w, layer) for w in weights],
        out_specs=[_whole_spec(o.shape) for o in out_shape],
        out_shape=out_shape,
        scratch_shapes=[pltpu.VMEM((n_seq, N_SLABS, HA + seq_len, LANES), F32),
                        pltpu.VMEM((n_seq, N_SLABS, HB + seq_len, LANES), F32),
                        pltpu.VMEM((N_SLABS, rows, LANES), F32),
                        pltpu.VMEM((rows, D), BF16),
                        pltpu.VMEM((rows, D), BF16),
                        pltpu.VMEM((rows, 3 * D), F32),
                        pltpu.VMEM((rows, D), F32),
                        pltpu.VMEM((rows, D), F32)],
        compiler_params=pltpu.CompilerParams(
            dimension_semantics=("arbitrary",), vmem_limit_bytes=VMEM_LIMIT),
        name="small_mixer",
    )(x, ha, hb, *weights)


def _front_pad(a, n_rows):
    pad = [(0, 0)] * a.ndim
    pad[-2] = (n_rows - a.shape[-2], 0)
    return jnp.pad(a, pad)


def kernel(x_prompt, x_sample, state_conv_a, state_conv_b, meta_tokens, norm1_g, w_in, conv_a_w, conv_a_b, ln_a_g, ln_a_b, w_a_out, conv_b_w, w_b_out, w_o, norm2_g, w_ffn_gate, w_ffn_up, w_ffn_down, final_norm_g):
    depth = w_in.shape[0]
    b_p, seq, _ = x_prompt.shape
    b_s, seq_s, _ = x_sample.shape
    assert seq % TM == 0 and seq_s == N_META and seq_s % BF16_ROWS == 0

    rows_of = lambda v: v.reshape(depth, 1, -1).astype(F32)
    params = dict(
        norm1_g=rows_of(norm1_g), conv_a_w=conv_a_w, conv_a_b=rows_of(conv_a_b),
        ln_a_g=rows_of(ln_a_g), ln_a_b=rows_of(ln_a_b), conv_b_w=conv_b_w,
        w_in=w_in.astype(BF16), w_a_out=w_a_out.astype(BF16), w_b_out=w_b_out.astype(BF16),
        w_o=w_o.astype(BF16), norm2_g=rows_of(norm2_g), w_ffn_gate=w_ffn_gate.astype(BF16),
        w_ffn_up=w_ffn_up.astype(BF16), w_ffn_down=w_ffn_down.astype(BF16),
        final_norm_g=final_norm_g.reshape(1, -1).astype(F32))

    xs = jnp.concatenate([meta_tokens.astype(F32), x_sample.reshape(b_s * seq_s, D)], axis=0)
    xp = x_prompt
    pa, pb, sa, sb = [], [], [], []
    for l in range(depth):
        last = l == depth - 1
        ha = _front_pad(jnp.concatenate([jnp.zeros((1, KA - 1, D), F32), state_conv_a[l]], axis=0), HA)
        hb = _front_pad(jnp.concatenate([jnp.zeros((1, KB - 1, D), F32), state_conv_b[l]], axis=0), HB)
        hs, na_s, nb_s = _small_mixer(xs, ha, hb, params, l)
        xs = _ffn_call(hs, params, l, last, hs.shape[0], "small_ffn")
        sa.append(na_s[1:])
        sb.append(nb_s[1:])
        h, na_p, nb_p = _prompt_mixer(xp, _front_pad(na_s[0], HA), _front_pad(nb_s[0], HB), params, l)
        pa.append(na_p)
        pb.append(nb_p)
        xp = _ffn_call(h.reshape(b_p * seq, D), params, l, last, TM, "prompt_ffn").reshape(b_p, seq, D)

    y_prompt = xp
    y_sample = xs[N_META:].reshape(b_s, seq_s, D)
    return (y_prompt, y_sample, jnp.stack(pa, axis=0), jnp.stack(pb, axis=0),
            jnp.stack(sa, axis=0), jnp.stack(sb, axis=0))
```

```python
import functools

import jax
import jax.numpy as jnp
from jax import lax
from jax.experimental import pallas as pl
from jax.experimental.pallas import tpu as pltpu

F32 = jnp.float32
BF16 = jnp.bfloat16

D = 1024
D_HID = 2816
KA = 31
KB = 3
N_META = 16
RMS_EPS = 1e-6
LN_EPS = 1e-5

LANES = 128
BF16_ROWS = 16
N_SLABS = D // LANES
MXU_COLS = 256
SLABS_PER_CHUNK = MXU_COLS // LANES
N_CHUNKS = D // MXU_COLS

HA = 32
HB = 8
OFF_A = HA - (KA - 1)
OFF_B = HB - (KB - 1)
TM = 512
CONV_ROWS = 64
CONV_B_ROWS = 128
CAST_STEPS = 16
ROW_UNROLL = 4
VMEM_LIMIT = 60000 * 1024

COL_A_VAL, COL_A_GATE, COL_B_B, COL_B_C, COL_B_X, COL_G_A, COL_G_B = (i * D for i in range(7))
N_IN = 7 * D


def _sigmoid(x):
    return 0.5 * jnp.tanh(0.5 * x) + 0.5


def _silu(x):
    h = 0.5 * x
    return h * jnp.tanh(h) + h


def _rmsnorm(x, g):
    return x * lax.rsqrt(jnp.mean(x * x, axis=-1, keepdims=True) + RMS_EPS) * g


def _layernorm_silu(c, g, b):
    mu = jnp.mean(c, axis=-1, keepdims=True)
    d = c - mu
    var = jnp.mean(d * d, axis=-1, keepdims=True)
    return _silu(d * lax.rsqrt(var + LN_EPS) * g + b)


def _row_blocks(n_rows, body):
    def step(i, carry):
        body(pl.multiple_of(i * BF16_ROWS, BF16_ROWS))
        return carry
    lax.fori_loop(0, n_rows // BF16_ROWS, step, 0, unroll=ROW_UNROLL)


def _slab_loop(body):
    def step(j, carry):
        body(j, pl.ds(pl.multiple_of(j * LANES, LANES), LANES))
        return carry
    lax.fori_loop(0, N_SLABS, step, 0)


def _lane_slab(j):
    return slice(j * LANES, (j + 1) * LANES)


def _dot(a, b):
    return jnp.dot(a, b, preferred_element_type=F32)


def _from_slabs(buf, rs):
    return jnp.concatenate([buf[j, rs, :] for j in range(N_SLABS)], axis=-1)


def _zero_after(v):
    bits = pltpu.bitcast(v, jnp.uint32)
    bits = lax.shift_right_logical(lax.shift_right_logical(bits, jnp.uint32(16)), jnp.uint32(16))
    return pltpu.bitcast(bits, F32)


def _conv_taps(load, store, w_ref, lanes, n_taps, off, rows, init, post=None, after=None):
    half = rows // 2
    acc = after
    for par in range(2):
        acc = init(half) if acc is None else init(half) + _zero_after(acc)
        for k in range(n_taps):
            acc = acc + load(par + k + off, half) * w_ref[k:k + 1, lanes]
        store(par, half, acc if post is None else post(par, half, acc))
    return acc


def _stage_rmsnorm_bf16(src_ref, g_ref, dst_ref, rows):
    def body(r):
        rs = pl.ds(r, BF16_ROWS)
        dst_ref[rs, :] = _rmsnorm(src_ref[rs, :], g_ref[...]).astype(BF16)
    _row_blocks(rows, body)


def _stage_layernorm_silu_bf16(c_buf, g_ref, b_ref, dst_ref, rows):
    def body(r):
        rs = pl.ds(r, BF16_ROWS)
        dst_ref[rs, :] = _layernorm_silu(_from_slabs(c_buf, rs), g_ref[...], b_ref[...]).astype(BF16)
    _row_blocks(rows, body)


def _stage_gate_b_bf16(p_ref, c_buf, dst_ref, rows):
    def body(r):
        rs = pl.ds(r, BF16_ROWS)
        dst_ref[rs, :] = (p_ref[rs, 0:D] * _from_slabs(c_buf, rs)).astype(BF16)
    _row_blocks(rows, body)


def _stage_merge_bf16(g_ref, ya_ref, yb_ref, dst_ref, rows):
    def body(r):
        rs = pl.ds(r, BF16_ROWS)
        m = _sigmoid(g_ref[rs, 0:D]) * ya_ref[rs, :] + _sigmoid(g_ref[rs, D:2 * D]) * yb_ref[rs, :]
        dst_ref[rs, :] = m.astype(BF16)
    _row_blocks(rows, body)


def _mixer_kernel(x_ref, ha0_ref, hb0_ref, n1g_ref, caw_ref, cab_ref, lng_ref, lnb_ref, cbw_ref,
                  w_in_ref, w_a_ref, w_b_ref, w_o_ref,
                  h_ref, na_ref, nb_ref,
                  ua_buf, zb_buf, bb_buf, c_buf, xn_buf, lhs_a, lhs_b, lhs_m, ga_buf, gb_buf, yb_buf):
    t = pl.program_id(1)
    last_t = pl.num_programs(1) - 1

    @pl.when(t == 0)
    def _():
        for j in range(N_SLABS):
            ua_buf[j, 0:HA, :] = ha0_ref[:, _lane_slab(j)]
            zb_buf[j, 0:HB, :] = hb0_ref[:, _lane_slab(j)]

    for r in range(0, TM, BF16_ROWS):
        xn_buf[r:r + BF16_ROWS, :] = _rmsnorm(x_ref[r:r + BF16_ROWS, :], n1g_ref[...]).astype(BF16)

    for c in range(N_CHUNKS):
        a_val = _dot(xn_buf[...], w_in_ref[:, COL_A_VAL + c * MXU_COLS:COL_A_VAL + (c + 1) * MXU_COLS])
        a_gate = _dot(xn_buf[...], w_in_ref[:, COL_A_GATE + c * MXU_COLS:COL_A_GATE + (c + 1) * MXU_COLS])
        u = a_val * _sigmoid(a_gate)
        for jj in range(SLABS_PER_CHUNK):
            ua_buf[c * SLABS_PER_CHUNK + jj, HA:HA + TM, :] = u[:, _lane_slab(jj)]

    def proj(col0, i):
        return _dot(xn_buf[...], w_in_ref[:, pl.ds(pl.multiple_of(col0 + i * MXU_COLS, MXU_COLS), MXU_COLS)])

    def conv_a_step(i, carry):
        cols = pl.ds(pl.multiple_of(i * MXU_COLS, MXU_COLS), MXU_COLS)
        rhs = jnp.concatenate(
            [w_in_ref[:, pl.ds(pl.multiple_of(col0 + i * MXU_COLS, MXU_COLS), MXU_COLS)]
             for col0 in (COL_B_C, COL_B_X, COL_G_A, COL_G_B)], axis=1)
        r = _dot(xn_buf[...], rhs)
        z = r[:, 0:MXU_COLS] * r[:, MXU_COLS:2 * MXU_COLS]
        for jj in range(SLABS_PER_CHUNK):
            zb_buf[i * SLABS_PER_CHUNK + jj, HB:HB + TM, :] = z[:, _lane_slab(jj)]
        ga_buf[:, cols] = r[:, 2 * MXU_COLS:3 * MXU_COLS]
        gb_buf[:, cols] = r[:, 3 * MXU_COLS:4 * MXU_COLS]
        acc = None
        for jj in range(SLABS_PER_CHUNK):
            j = i * SLABS_PER_CHUNK + jj
            lanes = pl.ds(pl.multiple_of(j * LANES, LANES), LANES)
            for r0 in range(0, TM, CONV_ROWS):
                def load(start, n, r0=r0, j=j):
                    return ua_buf[j, pl.ds(r0 + start, n, stride=2), :]
                def store(start, n, v, r0=r0, j=j):
                    c_buf[j, pl.ds(r0 + start, n, stride=2), :] = v
                acc = _conv_taps(load, store, caw_ref, lanes, KA, OFF_A, CONV_ROWS,
                                 lambda n, lanes=lanes: jnp.broadcast_to(cab_ref[:, lanes], (n, LANES)),
                                 after=acc)
        return carry
    lax.fori_loop(0, N_CHUNKS, conv_a_step, 0)


    for i in range(N_CHUNKS):
        b_b = proj(COL_B_B, i)
        for jj in range(SLABS_PER_CHUNK):
            bb_buf[i * SLABS_PER_CHUNK + jj, :, :] = b_b[:, _lane_slab(jj)]
        for jj in range(SLABS_PER_CHUNK):
            j = i * SLABS_PER_CHUNK + jj
            for r0 in range(0, TM, CONV_B_ROWS):
                def load(start, n, r0=r0, j=j):
                    return zb_buf[j, pl.ds(r0 + start, n, stride=2), :]
                def gate(par, n, acc, r0=r0, j=j):
                    return acc * bb_buf[j, pl.ds(r0 + par, n, stride=2), :]
                def store(start, n, v, r0=r0, j=j):
                    bb_buf[j, pl.ds(r0 + start, n, stride=2), :] = v
                _conv_taps(load, store, cbw_ref, _lane_slab(j), KB, OFF_B, CONV_B_ROWS,
                           lambda n: jnp.zeros((n, LANES), F32), post=gate)

    for r in range(0, TM, BF16_ROWS):
        lhs_b[r:r + BF16_ROWS, :] = _from_slabs(bb_buf, slice(r, r + BF16_ROWS)).astype(BF16)

    rows_per_chunk = TM // N_CHUNKS
    for c in range(N_CHUNKS):
        cols = slice(c * MXU_COLS, (c + 1) * MXU_COLS)
        yb_buf[:, cols] = _dot(lhs_b[...], w_b_ref[:, cols])
        for r in range(c * rows_per_chunk, (c + 1) * rows_per_chunk, BF16_ROWS):
            rs = slice(r, r + BF16_ROWS)
            lhs_a[rs, :] = _layernorm_silu(_from_slabs(c_buf, rs), lng_ref[...], lnb_ref[...]).astype(BF16)

    for c in range(N_CHUNKS):
        cols = slice(c * MXU_COLS, (c + 1) * MXU_COLS)
        y_a = _dot(lhs_a[...], w_a_ref[:, cols])
        lhs_m[:, cols] = (_sigmoid(ga_buf[:, cols]) * y_a
                          + _sigmoid(gb_buf[:, cols]) * yb_buf[:, cols]).astype(BF16)

    h_ref[...] = x_ref[...] + _dot(lhs_m[...], w_o_ref[...])

    @pl.when(t == last_t)
    def _():
        for j in range(N_SLABS):
            na_ref[:, _lane_slab(j)] = ua_buf[j, TM + OFF_A:TM + HA, :]
            nb_ref[:, _lane_slab(j)] = zb_buf[j, TM + OFF_B:TM + HB, :]
    ua_buf[:, 0:HA, :] = ua_buf[:, TM:TM + HA, :]
    zb_buf[:, 0:HB, :] = zb_buf[:, TM:TM + HB, :]


def _vmem_spec():
    return pl.BlockSpec(memory_space=pltpu.VMEM)


def _layer_spec(stacked, layer):
    zeros = (0,) * (stacked.ndim - 1)
    return pl.BlockSpec((None,) + stacked.shape[1:], lambda *_: (layer,) + zeros,
                        pipeline_mode=pl.Buffered(1))


MIXER_VECTORS = ("norm1_g", "conv_a_w", "conv_a_b", "ln_a_g", "ln_a_b", "conv_b_w")
MIXER_WEIGHTS = ("w_in", "w_a_out", "w_b_out", "w_o")
FFN_WEIGHTS = ("w_ffn_gate", "w_ffn_up", "w_ffn_down")


def _prompt_mixer(x, ha0, hb0, params, layer, bf16_weights):
    b, s, _ = x.shape
    row_spec = pl.BlockSpec((None, TM, D), lambda i, j: (i, j, 0))
    vectors = [params[k] for k in MIXER_VECTORS]
    weights = [bf16_weights[k] for k in MIXER_WEIGHTS]
    return pl.pallas_call(
        _mixer_kernel,
        grid=(b, s // TM),
        in_specs=([row_spec, _vmem_spec(), _vmem_spec()] + [_layer_spec(v, layer) for v in vectors]
                  + [_whole_spec(w.shape, single=True) for w in weights]),
        out_specs=[row_spec,
                   pl.BlockSpec((None, KA - 1, D), lambda i, j: (i, 0, 0)),
                   pl.BlockSpec((None, KB - 1, D), lambda i, j: (i, 0, 0))],
        out_shape=[jax.ShapeDtypeStruct((b, s, D), F32),
                   jax.ShapeDtypeStruct((b, KA - 1, D), F32),
                   jax.ShapeDtypeStruct((b, KB - 1, D), F32)],
        scratch_shapes=[pltpu.VMEM((N_SLABS, HA + TM, LANES), F32),
                        pltpu.VMEM((N_SLABS, HB + TM, LANES), F32),
                        pltpu.VMEM((N_SLABS, TM, LANES), F32),
                        pltpu.VMEM((N_SLABS, TM, LANES), F32),
                        pltpu.VMEM((TM, D), BF16),
                        pltpu.VMEM((TM, D), BF16),
                        pltpu.VMEM((TM, D), BF16),
                        pltpu.VMEM((TM, D), BF16),
                        pltpu.VMEM((TM, D), F32),
                        pltpu.VMEM((TM, D), F32),
                        pltpu.VMEM((TM, D), F32)],
        compiler_params=pltpu.CompilerParams(
            dimension_semantics=("arbitrary", "arbitrary"), vmem_limit_bytes=VMEM_LIMIT),
        name="prompt_mixer",
    )(x, ha0, hb0, *vectors, *weights)


def _small_ffn_kernel(final_norm, h_ref, n2g_ref, wg_f32, wu_f32, wd_f32, fg_ref,
                      out_ref, wg_out, wu_out, wd_out, hn_buf, f_buf, wg_ref, wu_ref, wd_ref):
    step = pl.program_id(0)

    @pl.when(step < CAST_STEPS)
    def _():
        _cast_chunk(step, (wg_f32, wu_f32, wd_f32), (wg_out, wu_out, wd_out), (wg_ref, wu_ref, wd_ref))

    @pl.when(step == CAST_STEPS)
    def _():
        _ffn_kernel(final_norm, h_ref, n2g_ref, wg_ref, wu_ref, wd_ref, fg_ref, out_ref, hn_buf, f_buf)


def _ffn_kernel(final_norm, h_ref, n2g_ref, wg_ref, wu_ref, wd_ref, fg_ref, out_ref, hn_buf, f_buf):
    rows = h_ref.shape[0]
    for r in range(0, rows, BF16_ROWS):
        hn_buf[r:r + BF16_ROWS, :] = _rmsnorm(h_ref[r:r + BF16_ROWS, :], n2g_ref[...]).astype(BF16)
    for c in range(0, D_HID, MXU_COLS):
        g = _dot(hn_buf[...], wg_ref[:, c:c + MXU_COLS])
        u = _dot(hn_buf[...], wu_ref[:, c:c + MXU_COLS])
        f_buf[:, c:c + MXU_COLS] = (_silu(g) * u).astype(BF16)
    out_ref[...] = h_ref[...] + _dot(f_buf[...], wd_ref[...])
    if final_norm:
        for r in range(0, rows, BF16_ROWS):
            out_ref[r:r + BF16_ROWS, :] = _rmsnorm(out_ref[r:r + BF16_ROWS, :], fg_ref[...])


def _prompt_ffn(h, params, layer, final_norm, bf16_weights):
    n = h.shape[0]
    row_spec = pl.BlockSpec((TM, D), lambda i: (i, 0))
    weights = [bf16_weights[k] for k in FFN_WEIGHTS]
    return pl.pallas_call(
        functools.partial(_ffn_kernel, final_norm),
        grid=(n // TM,),
        in_specs=([row_spec, _layer_spec(params["norm2_g"], layer)]
                  + [_whole_spec(w.shape, single=True) for w in weights] + [_vmem_spec()]),
        out_specs=row_spec,
        out_shape=jax.ShapeDtypeStruct((n, D), F32),
        scratch_shapes=[pltpu.VMEM((TM, D), BF16),
                        pltpu.VMEM((TM, D_HID), BF16)],
        compiler_params=pltpu.CompilerParams(
            dimension_semantics=("arbitrary",), vmem_limit_bytes=VMEM_LIMIT),
        name="prompt_ffn",
    )(h, params["norm2_g"], *weights, params["final_norm_g"])


def _small_ffn(h, params, layer, final_norm):
    rows = h.shape[0]
    weights = [params[k] for k in FFN_WEIGHTS]
    cast_outs = [_cast_out(w) for w in weights]
    outs = pl.pallas_call(
        functools.partial(_small_ffn_kernel, final_norm),
        grid=(CAST_STEPS + 1,),
        in_specs=([_whole_spec(h.shape, single=True), _layer_spec(params["norm2_g"], layer)]
                  + [_cast_in_spec(w, layer) for w in weights]
                  + [_whole_spec(params["final_norm_g"].shape, single=True)]),
        out_specs=[_whole_spec(h.shape)] + [spec for _, spec in cast_outs],
        out_shape=[jax.ShapeDtypeStruct((rows, D), F32)] + [shape for shape, _ in cast_outs],
        scratch_shapes=[pltpu.VMEM((rows, D), BF16),
                        pltpu.VMEM((rows, D_HID), BF16)]
                       + [pltpu.VMEM(w.shape[1:], BF16) for w in weights],
        compiler_params=pltpu.CompilerParams(
            dimension_semantics=("arbitrary",), vmem_limit_bytes=VMEM_LIMIT),
        name="small_ffn",
    )(h, params["norm2_g"], *weights, params["final_norm_g"])
    return outs[0], dict(zip(FFN_WEIGHTS, outs[1:]))


def _cast_chunk(step, f32_refs, out_refs, resident_refs):
    for f, o, r in zip(f32_refs, out_refs, resident_refs):
        k = f.shape[0]
        v = f[...].astype(BF16)
        o[...] = v
        r[pl.ds(pl.multiple_of(step * k, k), k), :] = v


def _small_kernel(n_seq, seq_len,
                  x_ref, ha_ref, hb_ref, n1g_ref, caw_ref, cab_ref, lng_ref, lnb_ref, cbw_ref,
                  w_in_f32, w_a_f32, w_b_f32, w_o_f32,
                  h_ref, na_ref, nb_ref, w_in_out, w_a_out, w_b_out, w_o_out,
                  ua_buf, zb_buf, c_buf, xn_buf, lhs_buf, p_buf, ya_buf, yb_buf,
                  w_in_ref, w_a_ref, w_b_ref, w_o_ref):
    step = pl.program_id(0)

    @pl.when(step < CAST_STEPS)
    def _():
        _cast_chunk(step, (w_in_f32, w_a_f32, w_b_f32, w_o_f32), (w_in_out, w_a_out, w_b_out, w_o_out),
                    (w_in_ref, w_a_ref, w_b_ref, w_o_ref))

    @pl.when(step == CAST_STEPS)
    def _():
        _small_mixer_body(n_seq, seq_len, x_ref, ha_ref, hb_ref, n1g_ref, caw_ref, cab_ref, lng_ref,
                          lnb_ref, cbw_ref, w_in_ref, w_a_ref, w_b_ref, w_o_ref, h_ref, na_ref, nb_ref,
                          ua_buf, zb_buf, c_buf, xn_buf, lhs_buf, p_buf, ya_buf, yb_buf)


def _small_mixer_body(n_seq, seq_len,
                      x_ref, ha_ref, hb_ref, n1g_ref, caw_ref, cab_ref, lng_ref, lnb_ref, cbw_ref,
                      w_in_ref, w_a_ref, w_b_ref, w_o_ref,
                      h_ref, na_ref, nb_ref,
                      ua_buf, zb_buf, c_buf, xn_buf, lhs_buf, p_buf, ya_buf, yb_buf):
    rows = n_seq * seq_len
    for j in range(N_SLABS):
        ua_buf[:, j, 0:HA, :] = ha_ref[:, :, _lane_slab(j)]
        zb_buf[:, j, 0:HB, :] = hb_ref[:, :, _lane_slab(j)]

    _stage_rmsnorm_bf16(x_ref, n1g_ref, xn_buf, rows)

    p_buf[:, 0:2 * D] = _dot(xn_buf[...], w_in_ref[:, COL_A_VAL:COL_B_B])
    for s in range(n_seq):
        rs = slice(s * seq_len, (s + 1) * seq_len)
        u = p_buf[rs, 0:D] * _sigmoid(p_buf[rs, D:2 * D])
        for j in range(N_SLABS):
            ua_buf[s, j, HA:HA + seq_len, :] = u[:, _lane_slab(j)]

    def conv_a(j, lanes):
        for s in range(n_seq):
            def load(start, n, s=s):
                return ua_buf[s, j, pl.ds(start, n, stride=2), :]
            def store(start, n, v, s=s):
                c_buf[j, pl.ds(s * seq_len + start, n, stride=2), :] = v
            _conv_taps(load, store, caw_ref, lanes, KA, OFF_A, seq_len,
                       lambda n: jnp.broadcast_to(cab_ref[:, lanes], (n, LANES)))
    _slab_loop(conv_a)
    for j in range(N_SLABS):
        na_ref[:, :, _lane_slab(j)] = ua_buf[:, j, seq_len + OFF_A:seq_len + HA, :]

    _stage_layernorm_silu_bf16(c_buf, lng_ref, lnb_ref, lhs_buf, rows)
    ya_buf[...] = _dot(lhs_buf[...], w_a_ref[...])

    p_buf[...] = _dot(xn_buf[...], w_in_ref[:, COL_B_B:COL_G_A])
    for s in range(n_seq):
        rs = slice(s * seq_len, (s + 1) * seq_len)
        z = p_buf[rs, D:2 * D] * p_buf[rs, 2 * D:3 * D]
        for j in range(N_SLABS):
            zb_buf[s, j, HB:HB + seq_len, :] = z[:, _lane_slab(j)]

    def conv_b(j, lanes):
        for s in range(n_seq):
            def load(start, n, s=s):
                return zb_buf[s, j, pl.ds(start, n, stride=2), :]
            def store(start, n, v, s=s):
                c_buf[j, pl.ds(s * seq_len + start, n, stride=2), :] = v
            _conv_taps(load, store, cbw_ref, lanes, KB, OFF_B, seq_len,
                       lambda n: jnp.zeros((n, LANES), F32))
    _slab_loop(conv_b)
    for j in range(N_SLABS):
        nb_ref[:, :, _lane_slab(j)] = zb_buf[:, j, seq_len + OFF_B:seq_len + HB, :]

    _stage_gate_b_bf16(p_buf, c_buf, lhs_buf, rows)
    yb_buf[...] = _dot(lhs_buf[...], w_b_ref[...])

    p_buf[:, 0:2 * D] = _dot(xn_buf[...], w_in_ref[:, COL_G_A:N_IN])
    _stage_merge_bf16(p_buf, ya_buf, yb_buf, lhs_buf, rows)
    h_ref[...] = x_ref[...] + _dot(lhs_buf[...], w_o_ref[...])


def _whole_spec(shape, single=False):
    zeros = (0,) * len(shape)
    if single:
        return pl.BlockSpec(shape, lambda *_: zeros, pipeline_mode=pl.Buffered(1))
    return pl.BlockSpec(shape, lambda *_: zeros)


def _cast_in_spec(stacked, layer):
    _, k, n = stacked.shape
    return pl.BlockSpec((None, k // CAST_STEPS, n), lambda s: (layer, jnp.minimum(s, CAST_STEPS - 1), 0))


def _cast_out(stacked):
    _, k, n = stacked.shape
    return (jax.ShapeDtypeStruct((k, n), BF16),
            pl.BlockSpec((k // CAST_STEPS, n), lambda s: (jnp.minimum(s, CAST_STEPS - 1), 0)))


def _small_mixer(x, ha, hb, params, layer):
    n_seq = ha.shape[0]
    rows = x.shape[0]
    seq_len = rows // n_seq
    vectors = [params[k] for k in MIXER_VECTORS]
    weights = [params[k] for k in MIXER_WEIGHTS]
    cast_outs = [_cast_out(w) for w in weights]
    out_shape = [jax.ShapeDtypeStruct((rows, D), F32),
                 jax.ShapeDtypeStruct((n_seq, KA - 1, D), F32),
                 jax.ShapeDtypeStruct((n_seq, KB - 1, D), F32)]
    outs = pl.pallas_call(
        functools.partial(_small_kernel, n_seq, seq_len),
        grid=(CAST_STEPS + 1,),
        in_specs=([_whole_spec(a.shape, single=True) for a in (x, ha, hb)]
                  + [_layer_spec(v, layer) for v in vectors] + [_cast_in_spec(w, layer) for w in weights]),
        out_specs=[_whole_spec(o.shape) for o in out_shape] + [spec for _, spec in cast_outs],
        out_shape=out_shape + [shape for shape, _ in cast_outs],
        scratch_shapes=[pltpu.VMEM((n_seq, N_SLABS, HA + seq_len, LANES), F32),
                        pltpu.VMEM((n_seq, N_SLABS, HB + seq_len, LANES), F32),
                        pltpu.VMEM((N_SLABS, rows, LANES), F32),
                        pltpu.VMEM((rows, D), BF16),
                        pltpu.VMEM((rows, D), BF16),
                        pltpu.VMEM((rows, 3 * D), F32),
                        pltpu.VMEM((rows, D), F32),
                        pltpu.VMEM((rows, D), F32)]
                       + [pltpu.VMEM(w.shape[1:], BF16) for w in weights],
        compiler_params=pltpu.CompilerParams(
            dimension_semantics=("arbitrary",), vmem_limit_bytes=VMEM_LIMIT),
        name="small_mixer",
    )(x, ha, hb, *vectors, *weights)
    return outs[:3], dict(zip(MIXER_WEIGHTS, outs[3:]))


def _front_pad(a, n_rows):
    pad = [(0, 0)] * a.ndim
    pad[-2] = (n_rows - a.shape[-2], 0)
    return jnp.pad(a, pad)


def kernel(x_prompt, x_sample, state_conv_a, state_conv_b, meta_tokens, norm1_g, w_in, conv_a_w, conv_a_b, ln_a_g, ln_a_b, w_a_out, conv_b_w, w_b_out, w_o, norm2_g, w_ffn_gate, w_ffn_up, w_ffn_down, final_norm_g):
    depth = w_in.shape[0]
    b_p, seq, _ = x_prompt.shape
    b_s, seq_s, _ = x_sample.shape
    assert seq % TM == 0 and seq_s == N_META and seq_s % BF16_ROWS == 0

    rows_of = lambda v: v.reshape(depth, 1, -1).astype(F32)
    params = dict(
        norm1_g=rows_of(norm1_g), conv_a_w=conv_a_w, conv_a_b=rows_of(conv_a_b),
        ln_a_g=rows_of(ln_a_g), ln_a_b=rows_of(ln_a_b), conv_b_w=conv_b_w,
        w_in=w_in, w_a_out=w_a_out, w_b_out=w_b_out, w_o=w_o, norm2_g=rows_of(norm2_g),
        w_ffn_gate=w_ffn_gate, w_ffn_up=w_ffn_up, w_ffn_down=w_ffn_down,
        final_norm_g=final_norm_g.reshape(1, -1).astype(F32))

    xs = jnp.concatenate([meta_tokens.astype(F32), x_sample.reshape(b_s * seq_s, D)], axis=0)
    xp = x_prompt
    pa, pb, sa, sb = [], [], [], []
    for l in range(depth):
        last = l == depth - 1
        ha = _front_pad(jnp.concatenate([jnp.zeros((1, KA - 1, D), F32), state_conv_a[l]], axis=0), HA)
        hb = _front_pad(jnp.concatenate([jnp.zeros((1, KB - 1, D), F32), state_conv_b[l]], axis=0), HB)
        (hs, na_s, nb_s), mixer_bf16 = _small_mixer(xs, ha, hb, params, l)
        xs, ffn_bf16 = _small_ffn(hs, params, l, last)
        sa.append(na_s[1:])
        sb.append(nb_s[1:])
        h, na_p, nb_p = _prompt_mixer(xp, _front_pad(na_s[0], HA), _front_pad(nb_s[0], HB), params, l,
                                      mixer_bf16)
        pa.append(na_p)
        pb.append(nb_p)
        xp = _prompt_ffn(h.reshape(b_p * seq, D), params, l, last, ffn_bf16).reshape(b_p, seq, D)

    y_prompt = xp
    y_sample = xs[N_META:].reshape(b_s, seq_s, D)
    return (y_prompt, y_sample, jnp.stack(pa, axis=0), jnp.stack(pb, axis=0),
            jnp.stack(sa, axis=0), jnp.stack(sb, axis=0))
```

```python
import functools

import jax
import jax.numpy as jnp
from jax import lax
from jax.experimental import pallas as pl
from jax.experimental.pallas import tpu as pltpu

F32 = jnp.float32
BF16 = jnp.bfloat16

D = 1024
D_HID = 2816
KA = 31
KB = 3
N_META = 16
RMS_EPS = 1e-6
LN_EPS = 1e-5

LANES = 128
BF16_ROWS = 16
N_SLABS = D // LANES
MXU_COLS = 256
SLABS_PER_CHUNK = MXU_COLS // LANES
N_CHUNKS = D // MXU_COLS

HA = 32
HB = 8
OFF_A = HA - (KA - 1)
OFF_B = HB - (KB - 1)
TM = 512
CONV_ROWS = 64
CONV_B_ROWS = 128
CAST_STEPS = 16
VMEM_LIMIT = 60000 * 1024

COL_A_VAL, COL_A_GATE, COL_B_B, COL_B_C, COL_B_X, COL_G_A, COL_G_B = (i * D for i in range(7))
N_IN = 7 * D


def _sigmoid(x):
    return 0.5 * jnp.tanh(0.5 * x) + 0.5


def _silu(x):
    h = 0.5 * x
    return h * jnp.tanh(h) + h


def _rmsnorm(x, g):
    return x * lax.rsqrt(jnp.mean(x * x, axis=-1, keepdims=True) + RMS_EPS) * g


def _layernorm_silu(c, g, b):
    mu = jnp.mean(c, axis=-1, keepdims=True)
    d = c - mu
    var = jnp.mean(d * d, axis=-1, keepdims=True)
    return _silu(d * lax.rsqrt(var + LN_EPS) * g + b)


def _row_blocks(n_rows, body):
    for r in range(0, n_rows, BF16_ROWS):
        body(r)


def _slab_loop(body):
    def step(j, carry):
        body(j, pl.ds(pl.multiple_of(j * LANES, LANES), LANES))
        return carry
    lax.fori_loop(0, N_SLABS, step, 0)


def _lane_slab(j):
    return slice(j * LANES, (j + 1) * LANES)


def _dot(a, b):
    return jnp.dot(a, b, preferred_element_type=F32)


def _from_slabs(buf, rs):
    return jnp.concatenate([buf[j, rs, :] for j in range(N_SLABS)], axis=-1)


def _zero_after(v):
    bits = pltpu.bitcast(v, jnp.uint32)
    bits = lax.shift_right_logical(lax.shift_right_logical(bits, jnp.uint32(16)), jnp.uint32(16))
    return pltpu.bitcast(bits, F32)


def _conv_taps(load, store, w_ref, lanes, n_taps, off, rows, init, post=None, after=None):
    half = rows // 2
    acc = after
    for par in range(2):
        acc = init(half) if acc is None else init(half) + _zero_after(acc)
        for k in range(n_taps):
            acc = acc + load(par + k + off, half) * w_ref[k:k + 1, lanes]
        store(par, half, acc if post is None else post(par, half, acc))
    return acc


def _stage_rmsnorm_bf16(src_ref, g_ref, dst_ref, rows):
    def body(r):
        rs = pl.ds(r, BF16_ROWS)
        dst_ref[rs, :] = _rmsnorm(src_ref[rs, :], g_ref[...]).astype(BF16)
    _row_blocks(rows, body)


def _stage_layernorm_silu_bf16(c_buf, g_ref, b_ref, dst_ref, rows):
    def body(r):
        rs = pl.ds(r, BF16_ROWS)
        dst_ref[rs, :] = _layernorm_silu(_from_slabs(c_buf, rs), g_ref[...], b_ref[...]).astype(BF16)
    _row_blocks(rows, body)


def _stage_gate_b_bf16(p_ref, c_buf, dst_ref, rows):
    def body(r):
        rs = pl.ds(r, BF16_ROWS)
        dst_ref[rs, :] = (p_ref[rs, 0:D] * _from_slabs(c_buf, rs)).astype(BF16)
    _row_blocks(rows, body)


def _stage_merge_bf16(g_ref, ya_ref, yb_ref, dst_ref, rows):
    def body(r):
        rs = pl.ds(r, BF16_ROWS)
        m = _sigmoid(g_ref[rs, 0:D]) * ya_ref[rs, :] + _sigmoid(g_ref[rs, D:2 * D]) * yb_ref[rs, :]
        dst_ref[rs, :] = m.astype(BF16)
    _row_blocks(rows, body)


def _mixer_kernel(x_ref, ha0_ref, hb0_ref, n1g_ref, caw_ref, cab_ref, lng_ref, lnb_ref, cbw_ref,
                  w_in_ref, w_a_ref, w_b_ref, w_o_ref,
                  h_ref, na_ref, nb_ref,
                  ua_buf, zb_buf, bb_buf, c_buf, xn_buf, lhs_a, lhs_b, lhs_m, ga_buf, gb_buf, yb_buf):
    t = pl.program_id(1)
    last_t = pl.num_programs(1) - 1

    @pl.when(t == 0)
    def _():
        for j in range(N_SLABS):
            ua_buf[j, 0:HA, :] = ha0_ref[:, _lane_slab(j)]
            zb_buf[j, 0:HB, :] = hb0_ref[:, _lane_slab(j)]

    for r in range(0, TM, BF16_ROWS):
        xn_buf[r:r + BF16_ROWS, :] = _rmsnorm(x_ref[r:r + BF16_ROWS, :], n1g_ref[...]).astype(BF16)

    for c in range(N_CHUNKS):
        a_val = _dot(xn_buf[...], w_in_ref[:, COL_A_VAL + c * MXU_COLS:COL_A_VAL + (c + 1) * MXU_COLS])
        a_gate = _dot(xn_buf[...], w_in_ref[:, COL_A_GATE + c * MXU_COLS:COL_A_GATE + (c + 1) * MXU_COLS])
        u = a_val * _sigmoid(a_gate)
        for jj in range(SLABS_PER_CHUNK):
            ua_buf[c * SLABS_PER_CHUNK + jj, HA:HA + TM, :] = u[:, _lane_slab(jj)]

    def proj(col0, i):
        return _dot(xn_buf[...], w_in_ref[:, pl.ds(pl.multiple_of(col0 + i * MXU_COLS, MXU_COLS), MXU_COLS)])

    def conv_a_step(i, carry):
        cols = pl.ds(pl.multiple_of(i * MXU_COLS, MXU_COLS), MXU_COLS)
        rhs = jnp.concatenate(
            [w_in_ref[:, pl.ds(pl.multiple_of(col0 + i * MXU_COLS, MXU_COLS), MXU_COLS)]
             for col0 in (COL_B_C, COL_B_X, COL_G_A, COL_G_B)], axis=1)
        r = _dot(xn_buf[...], rhs)
        z = r[:, 0:MXU_COLS] * r[:, MXU_COLS:2 * MXU_COLS]
        for jj in range(SLABS_PER_CHUNK):
            zb_buf[i * SLABS_PER_CHUNK + jj, HB:HB + TM, :] = z[:, _lane_slab(jj)]
        ga_buf[:, cols] = r[:, 2 * MXU_COLS:3 * MXU_COLS]
        gb_buf[:, cols] = r[:, 3 * MXU_COLS:4 * MXU_COLS]
        acc = None
        for jj in range(SLABS_PER_CHUNK):
            j = i * SLABS_PER_CHUNK + jj
            lanes = pl.ds(pl.multiple_of(j * LANES, LANES), LANES)
            for r0 in range(0, TM, CONV_ROWS):
                def load(start, n, r0=r0, j=j):
                    return ua_buf[j, pl.ds(r0 + start, n, stride=2), :]
                def store(start, n, v, r0=r0, j=j):
                    c_buf[j, pl.ds(r0 + start, n, stride=2), :] = v
                acc = _conv_taps(load, store, caw_ref, lanes, KA, OFF_A, CONV_ROWS,
                                 lambda n, lanes=lanes: jnp.broadcast_to(cab_ref[:, lanes], (n, LANES)),
                                 after=acc)
        return carry
    lax.fori_loop(0, N_CHUNKS, conv_a_step, 0)


    for i in range(N_CHUNKS):
        b_b = proj(COL_B_B, i)
        for jj in range(SLABS_PER_CHUNK):
            bb_buf[i * SLABS_PER_CHUNK + jj, :, :] = b_b[:, _lane_slab(jj)]
        for jj in range(SLABS_PER_CHUNK):
            j = i * SLABS_PER_CHUNK + jj
            for r0 in range(0, TM, CONV_B_ROWS):
                def load(start, n, r0=r0, j=j):
                    return zb_buf[j, pl.ds(r0 + start, n, stride=2), :]
                def gate(par, n, acc, r0=r0, j=j):
                    return acc * bb_buf[j, pl.ds(r0 + par, n, stride=2), :]
                def store(start, n, v, r0=r0, j=j):
                    bb_buf[j, pl.ds(r0 + start, n, stride=2), :] = v
                _conv_taps(load, store, cbw_ref, _lane_slab(j), KB, OFF_B, CONV_B_ROWS,
                           lambda n: jnp.zeros((n, LANES), F32), post=gate)

    for r in range(0, TM, BF16_ROWS):
        lhs_b[r:r + BF16_ROWS, :] = _from_slabs(bb_buf, slice(r, r + BF16_ROWS)).astype(BF16)

    rows_per_chunk = TM // N_CHUNKS
    for c in range(N_CHUNKS):
        cols = slice(c * MXU_COLS, (c + 1) * MXU_COLS)
        yb_buf[:, cols] = _dot(lhs_b[...], w_b_ref[:, cols])
        for r in range(c * rows_per_chunk, (c + 1) * rows_per_chunk, BF16_ROWS):
            rs = slice(r, r + BF16_ROWS)
            lhs_a[rs, :] = _layernorm_silu(_from_slabs(c_buf, rs), lng_ref[...], lnb_ref[...]).astype(BF16)

    for c in range(N_CHUNKS):
        cols = slice(c * MXU_COLS, (c + 1) * MXU_COLS)
        y_a = _dot(lhs_a[...], w_a_ref[:, cols])
        lhs_m[:, cols] = (_sigmoid(ga_buf[:, cols]) * y_a
                          + _sigmoid(gb_buf[:, cols]) * yb_buf[:, cols]).astype(BF16)

    h_ref[...] = x_ref[...] + _dot(lhs_m[...], w_o_ref[...])

    @pl.when(t == last_t)
    def _():
        for j in range(N_SLABS):
            na_ref[:, _lane_slab(j)] = ua_buf[j, TM + OFF_A:TM + HA, :]
            nb_ref[:, _lane_slab(j)] = zb_buf[j, TM + OFF_B:TM + HB, :]
    ua_buf[:, 0:HA, :] = ua_buf[:, TM:TM + HA, :]
    zb_buf[:, 0:HB, :] = zb_buf[:, TM:TM + HB, :]


def _vmem_spec():
    return pl.BlockSpec(memory_space=pltpu.VMEM)


def _layer_spec(stacked, layer):
    zeros = (0,) * (stacked.ndim - 1)
    return pl.BlockSpec((None,) + stacked.shape[1:], lambda *_: (layer,) + zeros,
                        pipeline_mode=pl.Buffered(1))


MIXER_VECTORS = ("norm1_g", "conv_a_w", "conv_a_b", "ln_a_g", "ln_a_b", "conv_b_w")
MIXER_WEIGHTS = ("w_in", "w_a_out", "w_b_out", "w_o")
FFN_WEIGHTS = ("w_ffn_gate", "w_ffn_up", "w_ffn_down")


def _prompt_mixer(x, ha0, hb0, params, layer, bf16_weights):
    b, s, _ = x.shape
    row_spec = pl.BlockSpec((None, TM, D), lambda i, j: (i, j, 0))
    vectors = [params[k] for k in MIXER_VECTORS]
    weights = [bf16_weights[k] for k in MIXER_WEIGHTS]
    return pl.pallas_call(
        _mixer_kernel,
        grid=(b, s // TM),
        in_specs=([row_spec, _vmem_spec(), _vmem_spec()] + [_layer_spec(v, layer) for v in vectors]
                  + [_whole_spec(w.shape, single=True) for w in weights]),
        out_specs=[row_spec,
                   pl.BlockSpec((None, KA - 1, D), lambda i, j: (i, 0, 0)),
                   pl.BlockSpec((None, KB - 1, D), lambda i, j: (i, 0, 0))],
        out_shape=[jax.ShapeDtypeStruct((b, s, D), F32),
                   jax.ShapeDtypeStruct((b, KA - 1, D), F32),
                   jax.ShapeDtypeStruct((b, KB - 1, D), F32)],
        scratch_shapes=[pltpu.VMEM((N_SLABS, HA + TM, LANES), F32),
                        pltpu.VMEM((N_SLABS, HB + TM, LANES), F32),
                        pltpu.VMEM((N_SLABS, TM, LANES), F32),
                        pltpu.VMEM((N_SLABS, TM, LANES), F32),
                        pltpu.VMEM((TM, D), BF16),
                        pltpu.VMEM((TM, D), BF16),
                        pltpu.VMEM((TM, D), BF16),
                        pltpu.VMEM((TM, D), BF16),
                        pltpu.VMEM((TM, D), F32),
                        pltpu.VMEM((TM, D), F32),
                        pltpu.VMEM((TM, D), F32)],
        compiler_params=pltpu.CompilerParams(
            dimension_semantics=("arbitrary", "arbitrary"), vmem_limit_bytes=VMEM_LIMIT),
        name="prompt_mixer",
    )(x, ha0, hb0, *vectors, *weights)


def _small_ffn_kernel(final_norm, h_ref, n2g_ref, wg_f32, wu_f32, wd_f32, fg_ref,
                      out_ref, wg_out, wu_out, wd_out, hn_buf, f_buf, wg_ref, wu_ref, wd_ref):
    step = pl.program_id(0)

    @pl.when(step < CAST_STEPS)
    def _():
        _cast_chunk(step, (wg_f32, wu_f32, wd_f32), (wg_out, wu_out, wd_out), (wg_ref, wu_ref, wd_ref))

    @pl.when(step == CAST_STEPS)
    def _():
        _ffn_kernel(final_norm, h_ref, n2g_ref, wg_ref, wu_ref, wd_ref, fg_ref, out_ref, hn_buf, f_buf)


def _ffn_kernel(final_norm, h_ref, n2g_ref, wg_ref, wu_ref, wd_ref, fg_ref, out_ref, hn_buf, f_buf):
    rows = h_ref.shape[0]
    for r in range(0, rows, BF16_ROWS):
        hn_buf[r:r + BF16_ROWS, :] = _rmsnorm(h_ref[r:r + BF16_ROWS, :], n2g_ref[...]).astype(BF16)
    for c in range(0, D_HID, MXU_COLS):
        g = _dot(hn_buf[...], wg_ref[:, c:c + MXU_COLS])
        u = _dot(hn_buf[...], wu_ref[:, c:c + MXU_COLS])
        f_buf[:, c:c + MXU_COLS] = (_silu(g) * u).astype(BF16)
    out_ref[...] = h_ref[...] + _dot(f_buf[...], wd_ref[...])
    if final_norm:
        for r in range(0, rows, BF16_ROWS):
            out_ref[r:r + BF16_ROWS, :] = _rmsnorm(out_ref[r:r + BF16_ROWS, :], fg_ref[...])


def _prompt_ffn(h, params, layer, final_norm, bf16_weights):
    n = h.shape[0]
    row_spec = pl.BlockSpec((TM, D), lambda i: (i, 0))
    weights = [bf16_weights[k] for k in FFN_WEIGHTS]
    return pl.pallas_call(
        functools.partial(_ffn_kernel, final_norm),
        grid=(n // TM,),
        in_specs=([row_spec, _layer_spec(params["norm2_g"], layer)]
                  + [_whole_spec(w.shape, single=True) for w in weights] + [_vmem_spec()]),
        out_specs=row_spec,
        out_shape=jax.ShapeDtypeStruct((n, D), F32),
        scratch_shapes=[pltpu.VMEM((TM, D), BF16),
                        pltpu.VMEM((TM, D_HID), BF16)],
        compiler_params=pltpu.CompilerParams(
            dimension_semantics=("arbitrary",), vmem_limit_bytes=VMEM_LIMIT),
        name="prompt_ffn",
    )(h, params["norm2_g"], *weights, params["final_norm_g"])


def _small_ffn(h, params, layer, final_norm):
    rows = h.shape[0]
    weights = [params[k] for k in FFN_WEIGHTS]
    cast_outs = [_cast_out(w) for w in weights]
    outs = pl.pallas_call(
        functools.partial(_small_ffn_kernel, final_norm),
        grid=(CAST_STEPS + 1,),
        in_specs=([_whole_spec(h.shape, single=True), _layer_spec(params["norm2_g"], layer)]
                  + [_cast_in_spec(w, layer) for w in weights]
                  + [_whole_spec(params["final_norm_g"].shape, single=True)]),
        out_specs=[_whole_spec(h.shape)] + [spec for _, spec in cast_outs],
        out_shape=[jax.ShapeDtypeStruct((rows, D), F32)] + [shape for shape, _ in cast_outs],
        scratch_shapes=[pltpu.VMEM((rows, D), BF16),
                        pltpu.VMEM((rows, D_HID), BF16)]
                       + [pltpu.VMEM(w.shape[1:], BF16) for w in weights],
        compiler_params=pltpu.CompilerParams(
            dimension_semantics=("arbitrary",), vmem_limit_bytes=VMEM_LIMIT),
        name="small_ffn",
    )(h, params["norm2_g"], *weights, params["final_norm_g"])
    return outs[0], dict(zip(FFN_WEIGHTS, outs[1:]))


def _cast_chunk(step, f32_refs, out_refs, resident_refs):
    for f, o, r in zip(f32_refs, out_refs, resident_refs):
        k = f.shape[0]
        v = f[...].astype(BF16)
        o[...] = v
        r[pl.ds(pl.multiple_of(step * k, k), k), :] = v


def _small_kernel(n_seq, seq_len,
                  x_ref, ha_ref, hb_ref, n1g_ref, caw_ref, cab_ref, lng_ref, lnb_ref, cbw_ref,
                  w_in_f32, w_a_f32, w_b_f32, w_o_f32,
                  h_ref, na_ref, nb_ref, w_in_out, w_a_out, w_b_out, w_o_out,
                  ua_buf, zb_buf, c_buf, xn_buf, lhs_buf, p_buf, ya_buf, yb_buf,
                  w_in_ref, w_a_ref, w_b_ref, w_o_ref):
    step = pl.program_id(0)

    @pl.when(step < CAST_STEPS)
    def _():
        _cast_chunk(step, (w_in_f32, w_a_f32, w_b_f32, w_o_f32), (w_in_out, w_a_out, w_b_out, w_o_out),
                    (w_in_ref, w_a_ref, w_b_ref, w_o_ref))

    @pl.when(step == CAST_STEPS)
    def _():
        _small_mixer_body(n_seq, seq_len, x_ref, ha_ref, hb_ref, n1g_ref, caw_ref, cab_ref, lng_ref,
                          lnb_ref, cbw_ref, w_in_ref, w_a_ref, w_b_ref, w_o_ref, h_ref, na_ref, nb_ref,
                          ua_buf, zb_buf, c_buf, xn_buf, lhs_buf, p_buf, ya_buf, yb_buf)


def _small_mixer_body(n_seq, seq_len,
                      x_ref, ha_ref, hb_ref, n1g_ref, caw_ref, cab_ref, lng_ref, lnb_ref, cbw_ref,
                      w_in_ref, w_a_ref, w_b_ref, w_o_ref,
                      h_ref, na_ref, nb_ref,
                      ua_buf, zb_buf, c_buf, xn_buf, lhs_buf, p_buf, ya_buf, yb_buf):
    rows = n_seq * seq_len
    for j in range(N_SLABS):
        ua_buf[:, j, 0:HA, :] = ha_ref[:, :, _lane_slab(j)]
        zb_buf[:, j, 0:HB, :] = hb_ref[:, :, _lane_slab(j)]

    _stage_rmsnorm_bf16(x_ref, n1g_ref, xn_buf, rows)

    p_buf[:, 0:2 * D] = _dot(xn_buf[...], w_in_ref[:, COL_A_VAL:COL_B_B])
    for s in range(n_seq):
        rs = slice(s * seq_len, (s + 1) * seq_len)
        u = p_buf[rs, 0:D] * _sigmoid(p_buf[rs, D:2 * D])
        for j in range(N_SLABS):
            ua_buf[s, j, HA:HA + seq_len, :] = u[:, _lane_slab(j)]

    def conv_a(j, lanes):
        for s in range(n_seq):
            def load(start, n, s=s):
                return ua_buf[s, j, pl.ds(start, n, stride=2), :]
            def store(start, n, v, s=s):
                c_buf[j, pl.ds(s * seq_len + start, n, stride=2), :] = v
            _conv_taps(load, store, caw_ref, lanes, KA, OFF_A, seq_len,
                       lambda n: jnp.broadcast_to(cab_ref[:, lanes], (n, LANES)))
    _slab_loop(conv_a)
    for j in range(N_SLABS):
        na_ref[:, :, _lane_slab(j)] = ua_buf[:, j, seq_len + OFF_A:seq_len + HA, :]

    _stage_layernorm_silu_bf16(c_buf, lng_ref, lnb_ref, lhs_buf, rows)
    ya_buf[...] = _dot(lhs_buf[...], w_a_ref[...])

    p_buf[...] = _dot(xn_buf[...], w_in_ref[:, COL_B_B:COL_G_A])
    for s in range(n_seq):
        rs = slice(s * seq_len, (s + 1) * seq_len)
        z = p_buf[rs, D:2 * D] * p_buf[rs, 2 * D:3 * D]
        for j in range(N_SLABS):
            zb_buf[s, j, HB:HB + seq_len, :] = z[:, _lane_slab(j)]

    def conv_b(j, lanes):
        for s in range(n_seq):
            def load(start, n, s=s):
                return zb_buf[s, j, pl.ds(start, n, stride=2), :]
            def store(start, n, v, s=s):
                c_buf[j, pl.ds(s * seq_len + start, n, stride=2), :] = v
            _conv_taps(load, store, cbw_ref, lanes, KB, OFF_B, seq_len,
                       lambda n: jnp.zeros((n, LANES), F32))
    _slab_loop(conv_b)
    for j in range(N_SLABS):
        nb_ref[:, :, _lane_slab(j)] = zb_buf[:, j, seq_len + OFF_B:seq_len + HB, :]

    _stage_gate_b_bf16(p_buf, c_buf, lhs_buf, rows)
    yb_buf[...] = _dot(lhs_buf[...], w_b_ref[...])

    p_buf[:, 0:2 * D] = _dot(xn_buf[...], w_in_ref[:, COL_G_A:N_IN])
    _stage_merge_bf16(p_buf, ya_buf, yb_buf, lhs_buf, rows)
    h_ref[...] = x_ref[...] + _dot(lhs_buf[...], w_o_ref[...])


def _whole_spec(shape, single=False):
    zeros = (0,) * len(shape)
    if single:
        return pl.BlockSpec(shape, lambda *_: zeros, pipeline_mode=pl.Buffered(1))
    return pl.BlockSpec(shape, lambda *_: zeros)


def _cast_in_spec(stacked, layer):
    _, k, n = stacked.shape
    return pl.BlockSpec((None, k // CAST_STEPS, n), lambda s: (layer, jnp.minimum(s, CAST_STEPS - 1), 0))


def _cast_out(stacked):
    _, k, n = stacked.shape
    return (jax.ShapeDtypeStruct((k, n), BF16),
            pl.BlockSpec((k // CAST_STEPS, n), lambda s: (jnp.minimum(s, CAST_STEPS - 1), 0)))


def _small_mixer(x, ha, hb, params, layer):
    n_seq = ha.shape[0]
    rows = x.shape[0]
    seq_len = rows // n_seq
    vectors = [params[k] for k in MIXER_VECTORS]
    weights = [params[k] for k in MIXER_WEIGHTS]
    cast_outs = [_cast_out(w) for w in weights]
    out_shape = [jax.ShapeDtypeStruct((rows, D), F32),
                 jax.ShapeDtypeStruct((n_seq, KA - 1, D), F32),
                 jax.ShapeDtypeStruct((n_seq, KB - 1, D), F32)]
    outs = pl.pallas_call(
        functools.partial(_small_kernel, n_seq, seq_len),
        grid=(CAST_STEPS + 1,),
        in_specs=([_whole_spec(a.shape, single=True) for a in (x, ha, hb)]
                  + [_layer_spec(v, layer) for v in vectors] + [_cast_in_spec(w, layer) for w in weights]),
        out_specs=[_whole_spec(o.shape) for o in out_shape] + [spec for _, spec in cast_outs],
        out_shape=out_shape + [shape for shape, _ in cast_outs],
        scratch_shapes=[pltpu.VMEM((n_seq, N_SLABS, HA + seq_len, LANES), F32),
                        pltpu.VMEM((n_seq, N_SLABS, HB + seq_len, LANES), F32),
                        pltpu.VMEM((N_SLABS, rows, LANES), F32),
                        pltpu.VMEM((rows, D), BF16),
                        pltpu.VMEM((rows, D), BF16),
                        pltpu.VMEM((rows, 3 * D), F32),
                        pltpu.VMEM((rows, D), F32),
                        pltpu.VMEM((rows, D), F32)]
                       + [pltpu.VMEM(w.shape[1:], BF16) for w in weights],
        compiler_params=pltpu.CompilerParams(
            dimension_semantics=("arbitrary",), vmem_limit_bytes=VMEM_LIMIT),
        name="small_mixer",
    )(x, ha, hb, *vectors, *weights)
    return outs[:3], dict(zip(MIXER_WEIGHTS, outs[3:]))


def _front_pad(a, n_rows):
    pad = [(0, 0)] * a.ndim
    pad[-2] = (n_rows - a.shape[-2], 0)
    return jnp.pad(a, pad)


def kernel(x_prompt, x_sample, state_conv_a, state_conv_b, meta_tokens, norm1_g, w_in, conv_a_w, conv_a_b, ln_a_g, ln_a_b, w_a_out, conv_b_w, w_b_out, w_o, norm2_g, w_ffn_gate, w_ffn_up, w_ffn_down, final_norm_g):
    depth = w_in.shape[0]
    b_p, seq, _ = x_prompt.shape
    b_s, seq_s, _ = x_sample.shape
    assert seq % TM == 0 and seq_s == N_META and seq_s % BF16_ROWS == 0

    rows_of = lambda v: v.reshape(depth, 1, -1).astype(F32)
    params = dict(
        norm1_g=rows_of(norm1_g), conv_a_w=conv_a_w, conv_a_b=rows_of(conv_a_b),
        ln_a_g=rows_of(ln_a_g), ln_a_b=rows_of(ln_a_b), conv_b_w=conv_b_w,
        w_in=w_in, w_a_out=w_a_out, w_b_out=w_b_out, w_o=w_o, norm2_g=rows_of(norm2_g),
        w_ffn_gate=w_ffn_gate, w_ffn_up=w_ffn_up, w_ffn_down=w_ffn_down,
        final_norm_g=final_norm_g.reshape(1, -1).astype(F32))

    xs = jnp.concatenate([meta_tokens.astype(F32), x_sample.reshape(b_s * seq_s, D)], axis=0)
    xp = x_prompt
    pa, pb, sa, sb = [], [], [], []
    for l in range(depth):
        last = l == depth - 1
        ha = _front_pad(jnp.concatenate([jnp.zeros((1, KA - 1, D), F32), state_conv_a[l]], axis=0), HA)
        hb = _front_pad(jnp.concatenate([jnp.zeros((1, KB - 1, D), F32), state_conv_b[l]], axis=0), HB)
        (hs, na_s, nb_s), mixer_bf16 = _small_mixer(xs, ha, hb, params, l)
        xs, ffn_bf16 = _small_ffn(hs, params, l, last)
        sa.append(na_s[1:])
        sb.append(nb_s[1:])
        h, na_p, nb_p = _prompt_mixer(xp, _front_pad(na_s[0], HA), _front_pad(nb_s[0], HB), params, l,
                                      mixer_bf16)
        pa.append(na_p)
        pb.append(nb_p)
        xp = _prompt_ffn(h.reshape(b_p * seq, D), params, l, last, ffn_bf16).reshape(b_p, seq, D)

    y_prompt = xp
    y_sample = xs[N_META:].reshape(b_s, seq_s, D)
    return (y_prompt, y_sample, jnp.stack(pa, axis=0), jnp.stack(pb, axis=0),
            jnp.stack(sa, axis=0), jnp.stack(sb, axis=0))
```

```python
import functools

import jax
import jax.numpy as jnp
from jax import lax
from jax.experimental import pallas as pl
from jax.experimental.pallas import tpu as pltpu

F32 = jnp.float32
BF16 = jnp.bfloat16

D = 1024
D_HID = 2816
KA = 31
KB = 3
N_META = 16
RMS_EPS = 1e-6
LN_EPS = 1e-5

LANES = 128
BF16_ROWS = 16
N_SLABS = D // LANES
MXU_COLS = 256
SLABS_PER_CHUNK = MXU_COLS // LANES
N_CHUNKS = D // MXU_COLS

HA = 32
HB = 8
OFF_A = HA - (KA - 1)
OFF_B = HB - (KB - 1)
TM = 512
CONV_ROWS = 64
CONV_B_ROWS = 128
CAST_STEPS = 16
VMEM_LIMIT = 60000 * 1024

COL_A_VAL, COL_A_GATE, COL_B_B, COL_B_C, COL_B_X, COL_G_A, COL_G_B = (i * D for i in range(7))
N_IN = 7 * D


def _sigmoid(x):
    return 0.5 * jnp.tanh(0.5 * x) + 0.5


def _silu(x):
    h = 0.5 * x
    return h * jnp.tanh(h) + h


def _rmsnorm(x, g):
    return x * lax.rsqrt(jnp.mean(x * x, axis=-1, keepdims=True) + RMS_EPS) * g


def _layernorm_silu(c, g, b):
    mu = jnp.mean(c, axis=-1, keepdims=True)
    d = c - mu
    var = jnp.mean(d * d, axis=-1, keepdims=True)
    return _silu(d * lax.rsqrt(var + LN_EPS) * g + b)


def _row_blocks(n_rows, body):
    for r in range(0, n_rows, BF16_ROWS):
        body(r)


def _slab_loop(body):
    def step(j, carry):
        body(j, pl.ds(pl.multiple_of(j * LANES, LANES), LANES))
        return carry
    lax.fori_loop(0, N_SLABS, step, 0)


def _lane_slab(j):
    return slice(j * LANES, (j + 1) * LANES)


def _dot(a, b):
    return jnp.dot(a, b, preferred_element_type=F32)


def _from_slabs(buf, rs):
    return jnp.concatenate([buf[j, rs, :] for j in range(N_SLABS)], axis=-1)


def _zero_after(v):
    bits = pltpu.bitcast(v, jnp.uint32)
    bits = lax.shift_right_logical(lax.shift_right_logical(bits, jnp.uint32(16)), jnp.uint32(16))
    return pltpu.bitcast(bits, F32)


def _conv_taps(load, store, w_ref, lanes, n_taps, off, rows, init, post=None, after=None):
    half = rows // 2
    acc = after
    for par in range(2):
        acc = init(half) if acc is None else init(half) + _zero_after(acc)
        for k in range(n_taps):
            acc = acc + load(par + k + off, half) * w_ref[k:k + 1, lanes]
        store(par, half, acc if post is None else post(par, half, acc))
    return acc


def _stage_rmsnorm_bf16(src_ref, g_ref, dst_ref, rows):
    def body(r):
        rs = pl.ds(r, BF16_ROWS)
        dst_ref[rs, :] = _rmsnorm(src_ref[rs, :], g_ref[...]).astype(BF16)
    _row_blocks(rows, body)


def _stage_layernorm_silu_bf16(c_buf, g_ref, b_ref, dst_ref, rows):
    def body(r):
        rs = pl.ds(r, BF16_ROWS)
        dst_ref[rs, :] = _layernorm_silu(_from_slabs(c_buf, rs), g_ref[...], b_ref[...]).astype(BF16)
    _row_blocks(rows, body)


def _stage_gate_b_bf16(p_ref, c_buf, dst_ref, rows):
    def body(r):
        rs = pl.ds(r, BF16_ROWS)
        dst_ref[rs, :] = (p_ref[rs, 0:D] * _from_slabs(c_buf, rs)).astype(BF16)
    _row_blocks(rows, body)


def _stage_merge_bf16(g_ref, ya_ref, yb_ref, dst_ref, rows):
    def body(r):
        rs = pl.ds(r, BF16_ROWS)
        m = _sigmoid(g_ref[rs, 0:D]) * ya_ref[rs, :] + _sigmoid(g_ref[rs, D:2 * D]) * yb_ref[rs, :]
        dst_ref[rs, :] = m.astype(BF16)
    _row_blocks(rows, body)


def _mixer_kernel(x_ref, ha0_ref, hb0_ref, n1g_ref, caw_ref, cab_ref, lng_ref, lnb_ref, cbw_ref,
                  w_in_ref, w_a_ref, w_b_ref, w_o_ref,
                  h_ref, na_ref, nb_ref,
                  ua_buf, zb_buf, bb_buf, c_buf, xn_buf, lhs_a, lhs_b, lhs_m, ga_buf, gb_buf, yb_buf):
    t = pl.program_id(1)
    last_t = pl.num_programs(1) - 1

    @pl.when(t == 0)
    def _():
        for j in range(N_SLABS):
            ua_buf[j, 0:HA, :] = ha0_ref[:, _lane_slab(j)]
            zb_buf[j, 0:HB, :] = hb0_ref[:, _lane_slab(j)]

    for r in range(0, TM, BF16_ROWS):
        xn_buf[r:r + BF16_ROWS, :] = _rmsnorm(x_ref[r:r + BF16_ROWS, :], n1g_ref[...]).astype(BF16)

    for c in range(N_CHUNKS):
        a_val = _dot(xn_buf[...], w_in_ref[:, COL_A_VAL + c * MXU_COLS:COL_A_VAL + (c + 1) * MXU_COLS])
        a_gate = _dot(xn_buf[...], w_in_ref[:, COL_A_GATE + c * MXU_COLS:COL_A_GATE + (c + 1) * MXU_COLS])
        u = a_val * _sigmoid(a_gate)
        for jj in range(SLABS_PER_CHUNK):
            ua_buf[c * SLABS_PER_CHUNK + jj, HA:HA + TM, :] = u[:, _lane_slab(jj)]

    def proj(col0, i):
        return _dot(xn_buf[...], w_in_ref[:, pl.ds(pl.multiple_of(col0 + i * MXU_COLS, MXU_COLS), MXU_COLS)])

    def conv_a_step(i, carry):
        cols = pl.ds(pl.multiple_of(i * MXU_COLS, MXU_COLS), MXU_COLS)
        rhs = jnp.concatenate(
            [w_in_ref[:, pl.ds(pl.multiple_of(col0 + i * MXU_COLS, MXU_COLS), MXU_COLS)]
             for col0 in (COL_B_C, COL_B_X, COL_G_A, COL_G_B)], axis=1)
        r = _dot(xn_buf[...], rhs)
        z = r[:, 0:MXU_COLS] * r[:, MXU_COLS:2 * MXU_COLS]
        for jj in range(SLABS_PER_CHUNK):
            zb_buf[i * SLABS_PER_CHUNK + jj, HB:HB + TM, :] = z[:, _lane_slab(jj)]
        ga_buf[:, cols] = r[:, 2 * MXU_COLS:3 * MXU_COLS]
        gb_buf[:, cols] = r[:, 3 * MXU_COLS:4 * MXU_COLS]
        acc = None
        for jj in range(SLABS_PER_CHUNK):
            j = i * SLABS_PER_CHUNK + jj
            lanes = pl.ds(pl.multiple_of(j * LANES, LANES), LANES)
            for r0 in range(0, TM, CONV_ROWS):
                def load(start, n, r0=r0, j=j):
                    return ua_buf[j, pl.ds(r0 + start, n, stride=2), :]
                def store(start, n, v, r0=r0, j=j):
                    c_buf[j, pl.ds(r0 + start, n, stride=2), :] = v
                acc = _conv_taps(load, store, caw_ref, lanes, KA, OFF_A, CONV_ROWS,
                                 lambda n, lanes=lanes: jnp.broadcast_to(cab_ref[:, lanes], (n, LANES)),
                                 after=acc)
        return carry
    lax.fori_loop(0, N_CHUNKS, conv_a_step, 0)


    for i in range(N_CHUNKS):
        b_b = proj(COL_B_B, i)
        for jj in range(SLABS_PER_CHUNK):
            bb_buf[i * SLABS_PER_CHUNK + jj, :, :] = b_b[:, _lane_slab(jj)]
        for jj in range(SLABS_PER_CHUNK):
            j = i * SLABS_PER_CHUNK + jj
            for r0 in range(0, TM, CONV_B_ROWS):
                def load(start, n, r0=r0, j=j):
                    return zb_buf[j, pl.ds(r0 + start, n, stride=2), :]
                def gate(par, n, acc, r0=r0, j=j):
                    return acc * bb_buf[j, pl.ds(r0 + par, n, stride=2), :]
                def store(start, n, v, r0=r0, j=j):
                    bb_buf[j, pl.ds(r0 + start, n, stride=2), :] = v
                _conv_taps(load, store, cbw_ref, _lane_slab(j), KB, OFF_B, CONV_B_ROWS,
                           lambda n: jnp.zeros((n, LANES), F32), post=gate)

    for r in range(0, TM, BF16_ROWS):
        lhs_b[r:r + BF16_ROWS, :] = _from_slabs(bb_buf, slice(r, r + BF16_ROWS)).astype(BF16)

    rows_per_chunk = TM // N_CHUNKS
    for c in range(N_CHUNKS):
        cols = slice(c * MXU_COLS, (c + 1) * MXU_COLS)
        yb_buf[:, cols] = _dot(lhs_b[...], w_b_ref[:, cols])
        for r in range(c * rows_per_chunk, (c + 1) * rows_per_chunk, BF16_ROWS):
            rs = slice(r, r + BF16_ROWS)
            lhs_a[rs, :] = _layernorm_silu(_from_slabs(c_buf, rs), lng_ref[...], lnb_ref[...]).astype(BF16)

    for c in range(N_CHUNKS):
        cols = slice(c * MXU_COLS, (c + 1) * MXU_COLS)
        y_a = _dot(lhs_a[...], w_a_ref[:, cols])
        lhs_m[:, cols] = (_sigmoid(ga_buf[:, cols]) * y_a
                          + _sigmoid(gb_buf[:, cols]) * yb_buf[:, cols]).astype(BF16)

    h_ref[...] = x_ref[...] + _dot(lhs_m[...], w_o_ref[...])

    @pl.when(t == last_t)
    def _():
        for j in range(N_SLABS):
            na_ref[:, _lane_slab(j)] = ua_buf[j, TM + OFF_A:TM + HA, :]
            nb_ref[:, _lane_slab(j)] = zb_buf[j, TM + OFF_B:TM + HB, :]
    ua_buf[:, 0:HA, :] = ua_buf[:, TM:TM + HA, :]
    zb_buf[:, 0:HB, :] = zb_buf[:, TM:TM + HB, :]


def _vmem_spec():
    return pl.BlockSpec(memory_space=pltpu.VMEM)


def _layer_spec(stacked, layer):
    zeros = (0,) * (stacked.ndim - 1)
    return pl.BlockSpec((None,) + stacked.shape[1:], lambda *_: (layer,) + zeros,
                        pipeline_mode=pl.Buffered(1))


MIXER_VECTORS = ("norm1_g", "conv_a_w", "conv_a_b", "ln_a_g", "ln_a_b", "conv_b_w")
MIXER_WEIGHTS = ("w_in", "w_a_out", "w_b_out", "w_o")
FFN_WEIGHTS = ("w_ffn_gate", "w_ffn_up", "w_ffn_down")


def _prompt_mixer(x, ha0, hb0, params, layer, bf16_weights):
    b, s, _ = x.shape
    row_spec = pl.BlockSpec((None, TM, D), lambda i, j: (i, j, 0))
    vectors = [params[k] for k in MIXER_VECTORS]
    weights = [bf16_weights[k] for k in MIXER_WEIGHTS]
    return pl.pallas_call(
        _mixer_kernel,
        grid=(b, s // TM),
        in_specs=([row_spec, _vmem_spec(), _vmem_spec()] + [_layer_spec(v, layer) for v in vectors]
                  + [_whole_spec(w.shape, single=True) for w in weights]),
        out_specs=[row_spec,
                   pl.BlockSpec((None, KA - 1, D), lambda i, j: (i, 0, 0)),
                   pl.BlockSpec((None, KB - 1, D), lambda i, j: (i, 0, 0))],
        out_shape=[jax.ShapeDtypeStruct((b, s, D), F32),
                   jax.ShapeDtypeStruct((b, KA - 1, D), F32),
                   jax.ShapeDtypeStruct((b, KB - 1, D), F32)],
        scratch_shapes=[pltpu.VMEM((N_SLABS, HA + TM, LANES), F32),
                        pltpu.VMEM((N_SLABS, HB + TM, LANES), F32),
                        pltpu.VMEM((N_SLABS, TM, LANES), F32),
                        pltpu.VMEM((N_SLABS, TM, LANES), F32),
                        pltpu.VMEM((TM, D), BF16),
                        pltpu.VMEM((TM, D), BF16),
                        pltpu.VMEM((TM, D), BF16),
                        pltpu.VMEM((TM, D), BF16),
                        pltpu.VMEM((TM, D), F32),
                        pltpu.VMEM((TM, D), F32),
                        pltpu.VMEM((TM, D), F32)],
        compiler_params=pltpu.CompilerParams(
            dimension_semantics=("arbitrary", "arbitrary"), vmem_limit_bytes=VMEM_LIMIT),
        name="prompt_mixer",
    )(x, ha0, hb0, *vectors, *weights)


def _small_ffn_kernel(final_norm, h_ref, n2g_ref, wg_f32, wu_f32, wd_f32, fg_ref,
                      out_ref, wg_out, wu_out, wd_out, hn_buf, f_buf, wg_ref, wu_ref, wd_ref):
    step = pl.program_id(0)

    @pl.when(step < CAST_STEPS)
    def _():
        _cast_chunk(step, (wg_f32, wu_f32, wd_f32), (wg_out, wu_out, wd_out), (wg_ref, wu_ref, wd_ref))

    @pl.when(step == CAST_STEPS)
    def _():
        _ffn_kernel(final_norm, h_ref, n2g_ref, wg_ref, wu_ref, wd_ref, fg_ref, out_ref, hn_buf, f_buf)


def _ffn_kernel(final_norm, h_ref, n2g_ref, wg_ref, wu_ref, wd_ref, fg_ref, out_ref, hn_buf, f_buf):
    rows = h_ref.shape[0]
    for r in range(0, rows, BF16_ROWS):
        hn_buf[r:r + BF16_ROWS, :] = _rmsnorm(h_ref[r:r + BF16_ROWS, :], n2g_ref[...]).astype(BF16)
    for c in range(0, D_HID, MXU_COLS):
        g = _dot(hn_buf[...], wg_ref[:, c:c + MXU_COLS])
        u = _dot(hn_buf[...], wu_ref[:, c:c + MXU_COLS])
        f_buf[:, c:c + MXU_COLS] = (_silu(g) * u).astype(BF16)
    out_ref[...] = h_ref[...] + _dot(f_buf[...], wd_ref[...])
    if final_norm:
        for r in range(0, rows, BF16_ROWS):
            out_ref[r:r + BF16_ROWS, :] = _rmsnorm(out_ref[r:r + BF16_ROWS, :], fg_ref[...])


def _prompt_ffn_kernel(final_norm, n_cast, h_ref, n2g_ref, wg_ref, wu_ref, wd_ref, fg_ref, *rest):
    f32_refs, out_ref, cast_outs = rest[:n_cast], rest[n_cast], rest[n_cast + 1:2 * n_cast + 1]
    hn_buf, f_buf = rest[2 * n_cast + 1:]
    if n_cast:
        @pl.when(pl.program_id(0) < CAST_STEPS)
        def _():
            for f, o in zip(f32_refs, cast_outs):
                o[...] = f[...].astype(BF16)
    _ffn_kernel(final_norm, h_ref, n2g_ref, wg_ref, wu_ref, wd_ref, fg_ref, out_ref, hn_buf, f_buf)


def _prompt_ffn(h, params, layer, final_norm, bf16_weights, cast_layer=None):
    n = h.shape[0]
    assert n // TM >= CAST_STEPS
    row_spec = pl.BlockSpec((TM, D), lambda i: (i, 0))
    weights = [bf16_weights[k] for k in FFN_WEIGHTS]
    cast_names = () if cast_layer is None else MIXER_WEIGHTS + FFN_WEIGHTS
    cast_ins = [params[k] for k in cast_names]
    cast_outs = [_cast_out(w) for w in cast_ins]
    outs = pl.pallas_call(
        functools.partial(_prompt_ffn_kernel, final_norm, len(cast_names)),
        grid=(n // TM,),
        in_specs=([row_spec, _layer_spec(params["norm2_g"], layer)]
                  + [_whole_spec(w.shape, single=True) for w in weights] + [_vmem_spec()]
                  + [_cast_in_spec(w, cast_layer) for w in cast_ins]),
        out_specs=[row_spec] + [spec for _, spec in cast_outs],
        out_shape=[jax.ShapeDtypeStruct((n, D), F32)] + [shape for shape, _ in cast_outs],
        scratch_shapes=[pltpu.VMEM((TM, D), BF16),
                        pltpu.VMEM((TM, D_HID), BF16)],
        compiler_params=pltpu.CompilerParams(
            dimension_semantics=("arbitrary",), vmem_limit_bytes=VMEM_LIMIT),
        name="prompt_ffn",
    )(h, params["norm2_g"], *weights, params["final_norm_g"], *cast_ins)
    return outs[0], dict(zip(cast_names, outs[1:]))


def _small_ffn(h, params, layer, final_norm, bf16_weights=None):
    rows = h.shape[0]
    if bf16_weights is not None:
        ready = [bf16_weights[k] for k in FFN_WEIGHTS]
        fg = params["final_norm_g"]
        out = pl.pallas_call(
            functools.partial(_ffn_kernel, final_norm),
            grid=(1,),
            in_specs=([_whole_spec(h.shape, single=True), _layer_spec(params["norm2_g"], layer)]
                      + [_whole_spec(w.shape, single=True) for w in ready] + [_whole_spec(fg.shape, single=True)]),
            out_specs=_whole_spec(h.shape),
            out_shape=jax.ShapeDtypeStruct((rows, D), F32),
            scratch_shapes=[pltpu.VMEM((rows, D), BF16), pltpu.VMEM((rows, D_HID), BF16)],
            compiler_params=pltpu.CompilerParams(
                dimension_semantics=("arbitrary",), vmem_limit_bytes=VMEM_LIMIT),
            name="small_ffn",
        )(h, params["norm2_g"], *ready, fg)
        return out, bf16_weights
    weights = [params[k] for k in FFN_WEIGHTS]
    cast_outs = [_cast_out(w) for w in weights]
    outs = pl.pallas_call(
        functools.partial(_small_ffn_kernel, final_norm),
        grid=(CAST_STEPS + 1,),
        in_specs=([_whole_spec(h.shape, single=True), _layer_spec(params["norm2_g"], layer)]
                  + [_cast_in_spec(w, layer) for w in weights]
                  + [_whole_spec(params["final_norm_g"].shape, single=True)]),
        out_specs=[_whole_spec(h.shape)] + [spec for _, spec in cast_outs],
        out_shape=[jax.ShapeDtypeStruct((rows, D), F32)] + [shape for shape, _ in cast_outs],
        scratch_shapes=[pltpu.VMEM((rows, D), BF16),
                        pltpu.VMEM((rows, D_HID), BF16)]
                       + [pltpu.VMEM(w.shape[1:], BF16) for w in weights],
        compiler_params=pltpu.CompilerParams(
            dimension_semantics=("arbitrary",), vmem_limit_bytes=VMEM_LIMIT),
        name="small_ffn",
    )(h, params["norm2_g"], *weights, params["final_norm_g"])
    return outs[0], dict(zip(FFN_WEIGHTS, outs[1:]))


def _cast_chunk(step, f32_refs, out_refs, resident_refs):
    for f, o, r in zip(f32_refs, out_refs, resident_refs):
        k = f.shape[0]
        v = f[...].astype(BF16)
        o[...] = v
        r[pl.ds(pl.multiple_of(step * k, k), k), :] = v


def _small_kernel(n_seq, seq_len,
                  x_ref, ha_ref, hb_ref, n1g_ref, caw_ref, cab_ref, lng_ref, lnb_ref, cbw_ref,
                  w_in_f32, w_a_f32, w_b_f32, w_o_f32,
                  h_ref, na_ref, nb_ref, w_in_out, w_a_out, w_b_out, w_o_out,
                  ua_buf, zb_buf, c_buf, xn_buf, lhs_buf, p_buf, ya_buf, yb_buf,
                  w_in_ref, w_a_ref, w_b_ref, w_o_ref):
    step = pl.program_id(0)

    @pl.when(step < CAST_STEPS)
    def _():
        _cast_chunk(step, (w_in_f32, w_a_f32, w_b_f32, w_o_f32), (w_in_out, w_a_out, w_b_out, w_o_out),
                    (w_in_ref, w_a_ref, w_b_ref, w_o_ref))

    @pl.when(step == CAST_STEPS)
    def _():
        _small_mixer_body(n_seq, seq_len, x_ref, ha_ref, hb_ref, n1g_ref, caw_ref, cab_ref, lng_ref,
                          lnb_ref, cbw_ref, w_in_ref, w_a_ref, w_b_ref, w_o_ref, h_ref, na_ref, nb_ref,
                          ua_buf, zb_buf, c_buf, xn_buf, lhs_buf, p_buf, ya_buf, yb_buf)


def _small_mixer_body(n_seq, seq_len,
                      x_ref, ha_ref, hb_ref, n1g_ref, caw_ref, cab_ref, lng_ref, lnb_ref, cbw_ref,
                      w_in_ref, w_a_ref, w_b_ref, w_o_ref,
                      h_ref, na_ref, nb_ref,
                      ua_buf, zb_buf, c_buf, xn_buf, lhs_buf, p_buf, ya_buf, yb_buf):
    rows = n_seq * seq_len
    for j in range(N_SLABS):
        ua_buf[:, j, 0:HA, :] = ha_ref[:, :, _lane_slab(j)]
        zb_buf[:, j, 0:HB, :] = hb_ref[:, :, _lane_slab(j)]

    _stage_rmsnorm_bf16(x_ref, n1g_ref, xn_buf, rows)

    p_buf[:, 0:2 * D] = _dot(xn_buf[...], w_in_ref[:, COL_A_VAL:COL_B_B])
    for s in range(n_seq):
        rs = slice(s * seq_len, (s + 1) * seq_len)
        u = p_buf[rs, 0:D] * _sigmoid(p_buf[rs, D:2 * D])
        for j in range(N_SLABS):
            ua_buf[s, j, HA:HA + seq_len, :] = u[:, _lane_slab(j)]

    def conv_a(j, lanes):
        for s in range(n_seq):
            def load(start, n, s=s):
                return ua_buf[s, j, pl.ds(start, n, stride=2), :]
            def store(start, n, v, s=s):
                c_buf[j, pl.ds(s * seq_len + start, n, stride=2), :] = v
            _conv_taps(load, store, caw_ref, lanes, KA, OFF_A, seq_len,
                       lambda n: jnp.broadcast_to(cab_ref[:, lanes], (n, LANES)))
    _slab_loop(conv_a)
    for j in range(N_SLABS):
        na_ref[:, :, _lane_slab(j)] = ua_buf[:, j, seq_len + OFF_A:seq_len + HA, :]

    _stage_layernorm_silu_bf16(c_buf, lng_ref, lnb_ref, lhs_buf, rows)
    ya_buf[...] = _dot(lhs_buf[...], w_a_ref[...])

    p_buf[...] = _dot(xn_buf[...], w_in_ref[:, COL_B_B:COL_G_A])
    for s in range(n_seq):
        rs = slice(s * seq_len, (s + 1) * seq_len)
        z = p_buf[rs, D:2 * D] * p_buf[rs, 2 * D:3 * D]
        for j in range(N_SLABS):
            zb_buf[s, j, HB:HB + seq_len, :] = z[:, _lane_slab(j)]

    def conv_b(j, lanes):
        for s in range(n_seq):
            def load(start, n, s=s):
                return zb_buf[s, j, pl.ds(start, n, stride=2), :]
            def store(start, n, v, s=s):
                c_buf[j, pl.ds(s * seq_len + start, n, stride=2), :] = v
            _conv_taps(load, store, cbw_ref, lanes, KB, OFF_B, seq_len,
                       lambda n: jnp.zeros((n, LANES), F32))
    _slab_loop(conv_b)
    for j in range(N_SLABS):
        nb_ref[:, :, _lane_slab(j)] = zb_buf[:, j, seq_len + OFF_B:seq_len + HB, :]

    _stage_gate_b_bf16(p_buf, c_buf, lhs_buf, rows)
    yb_buf[...] = _dot(lhs_buf[...], w_b_ref[...])

    p_buf[:, 0:2 * D] = _dot(xn_buf[...], w_in_ref[:, COL_G_A:N_IN])
    _stage_merge_bf16(p_buf, ya_buf, yb_buf, lhs_buf, rows)
    h_ref[...] = x_ref[...] + _dot(lhs_buf[...], w_o_ref[...])


def _whole_spec(shape, single=False):
    zeros = (0,) * len(shape)
    if single:
        return pl.BlockSpec(shape, lambda *_: zeros, pipeline_mode=pl.Buffered(1))
    return pl.BlockSpec(shape, lambda *_: zeros)


def _cast_in_spec(stacked, layer):
    _, k, n = stacked.shape
    return pl.BlockSpec((None, k // CAST_STEPS, n), lambda s: (layer, jnp.minimum(s, CAST_STEPS - 1), 0))


def _cast_out(stacked):
    _, k, n = stacked.shape
    return (jax.ShapeDtypeStruct((k, n), BF16),
            pl.BlockSpec((k // CAST_STEPS, n), lambda s: (jnp.minimum(s, CAST_STEPS - 1), 0)))


def _small_mixer(x, ha, hb, params, layer, bf16_weights=None):
    n_seq = ha.shape[0]
    rows = x.shape[0]
    seq_len = rows // n_seq
    vectors = [params[k] for k in MIXER_VECTORS]
    out_shape = [jax.ShapeDtypeStruct((rows, D), F32),
                 jax.ShapeDtypeStruct((n_seq, KA - 1, D), F32),
                 jax.ShapeDtypeStruct((n_seq, KB - 1, D), F32)]
    data_specs = ([_whole_spec(a.shape, single=True) for a in (x, ha, hb)]
                  + [_layer_spec(v, layer) for v in vectors])
    scratch = [pltpu.VMEM((n_seq, N_SLABS, HA + seq_len, LANES), F32),
               pltpu.VMEM((n_seq, N_SLABS, HB + seq_len, LANES), F32),
               pltpu.VMEM((N_SLABS, rows, LANES), F32),
               pltpu.VMEM((rows, D), BF16),
               pltpu.VMEM((rows, D), BF16),
               pltpu.VMEM((rows, 3 * D), F32),
               pltpu.VMEM((rows, D), F32),
               pltpu.VMEM((rows, D), F32)]
    compiler_params = pltpu.CompilerParams(dimension_semantics=("arbitrary",), vmem_limit_bytes=VMEM_LIMIT)
    if bf16_weights is not None:
        ready = [bf16_weights[k] for k in MIXER_WEIGHTS]
        outs = pl.pallas_call(
            functools.partial(_small_mixer_body, n_seq, seq_len),
            grid=(1,),
            in_specs=data_specs + [_whole_spec(w.shape, single=True) for w in ready],
            out_specs=[_whole_spec(o.shape) for o in out_shape],
            out_shape=out_shape,
            scratch_shapes=scratch,
            compiler_params=compiler_params,
            name="small_mixer",
        )(x, ha, hb, *vectors, *ready)
        return outs, bf16_weights
    weights = [params[k] for k in MIXER_WEIGHTS]
    cast_outs = [_cast_out(w) for w in weights]
    outs = pl.pallas_call(
        functools.partial(_small_kernel, n_seq, seq_len),
        grid=(CAST_STEPS + 1,),
        in_specs=data_specs + [_cast_in_spec(w, layer) for w in weights],
        out_specs=[_whole_spec(o.shape) for o in out_shape] + [spec for _, spec in cast_outs],
        out_shape=out_shape + [shape for shape, _ in cast_outs],
        scratch_shapes=scratch + [pltpu.VMEM(w.shape[1:], BF16) for w in weights],
        compiler_params=compiler_params,
        name="small_mixer",
    )(x, ha, hb, *vectors, *weights)
    return outs[:3], dict(zip(MIXER_WEIGHTS, outs[3:]))


def _front_pad(a, n_rows):
    pad = [(0, 0)] * a.ndim
    pad[-2] = (n_rows - a.shape[-2], 0)
    return jnp.pad(a, pad)


def kernel(x_prompt, x_sample, state_conv_a, state_conv_b, meta_tokens, norm1_g, w_in, conv_a_w, conv_a_b, ln_a_g, ln_a_b, w_a_out, conv_b_w, w_b_out, w_o, norm2_g, w_ffn_gate, w_ffn_up, w_ffn_down, final_norm_g):
    depth = w_in.shape[0]
    b_p, seq, _ = x_prompt.shape
    b_s, seq_s, _ = x_sample.shape
    assert seq % TM == 0 and seq_s == N_META and seq_s % BF16_ROWS == 0

    rows_of = lambda v: v.reshape(depth, 1, -1).astype(F32)
    params = dict(
        norm1_g=rows_of(norm1_g), conv_a_w=conv_a_w, conv_a_b=rows_of(conv_a_b),
        ln_a_g=rows_of(ln_a_g), ln_a_b=rows_of(ln_a_b), conv_b_w=conv_b_w,
        w_in=w_in, w_a_out=w_a_out, w_b_out=w_b_out, w_o=w_o, norm2_g=rows_of(norm2_g),
        w_ffn_gate=w_ffn_gate, w_ffn_up=w_ffn_up, w_ffn_down=w_ffn_down,
        final_norm_g=final_norm_g.reshape(1, -1).astype(F32))

    xs = jnp.concatenate([meta_tokens.astype(F32), x_sample.reshape(b_s * seq_s, D)], axis=0)
    xp = x_prompt
    pa, pb, sa, sb = [], [], [], []
    layer_bf16 = None
    for l in range(depth):
        last = l == depth - 1
        ha = _front_pad(jnp.concatenate([jnp.zeros((1, KA - 1, D), F32), state_conv_a[l]], axis=0), HA)
        hb = _front_pad(jnp.concatenate([jnp.zeros((1, KB - 1, D), F32), state_conv_b[l]], axis=0), HB)
        (hs, na_s, nb_s), mixer_bf16 = _small_mixer(xs, ha, hb, params, l, layer_bf16)
        xs, ffn_bf16 = _small_ffn(hs, params, l, last, layer_bf16)
        sa.append(na_s[1:])
        sb.append(nb_s[1:])
        h, na_p, nb_p = _prompt_mixer(xp, _front_pad(na_s[0], HA), _front_pad(nb_s[0], HB), params, l,
                                      mixer_bf16)
        pa.append(na_p)
        pb.append(nb_p)
        xp, layer_bf16 = _prompt_ffn(h.reshape(b_p * seq, D), params, l, last, ffn_bf16,
                                     cast_layer=None if last else l + 1)
        xp = xp.reshape(b_p, seq, D)
        layer_bf16 = layer_bf16 or None

    y_prompt = xp
    y_sample = xs[N_META:].reshape(b_s, seq_s, D)
    return (y_prompt, y_sample, jnp.stack(pa, axis=0), jnp.stack(pb, axis=0),
            jnp.stack(sa, axis=0), jnp.stack(sb, axis=0))
```

```python
import functools

import jax
import jax.numpy as jnp
from jax import lax
from jax.experimental import pallas as pl
from jax.experimental.pallas import tpu as pltpu

F32 = jnp.float32
BF16 = jnp.bfloat16

D = 1024
D_HID = 2816
KA = 31
KB = 3
N_META = 16
RMS_EPS = 1e-6
LN_EPS = 1e-5

LANES = 128
BF16_ROWS = 16
N_SLABS = D // LANES
MXU_COLS = 256
SLABS_PER_CHUNK = MXU_COLS // LANES
N_CHUNKS = D // MXU_COLS

HA = 32
HB = 32
OFF_A = HA - (KA - 1)
OFF_B = HB - (KB - 1)
TM = 512
CONV_ROWS = 64
CONV_B_ROWS = 128
CAST_STEPS = 16
VMEM_LIMIT = 60000 * 1024

COL_A_VAL, COL_A_GATE, COL_B_B, COL_B_C, COL_B_X, COL_G_A, COL_G_B = (i * D for i in range(7))
N_IN = 7 * D


def _sigmoid(x):
    return 0.5 * jnp.tanh(0.5 * x) + 0.5


def _silu(x):
    h = 0.5 * x
    return h * jnp.tanh(h) + h


def _rmsnorm(x, g):
    return x * lax.rsqrt(jnp.mean(x * x, axis=-1, keepdims=True) + RMS_EPS) * g


def _layernorm_silu(c, g, b):
    mu = jnp.mean(c, axis=-1, keepdims=True)
    d = c - mu
    var = jnp.mean(d * d, axis=-1, keepdims=True)
    return _silu(d * lax.rsqrt(var + LN_EPS) * g + b)


def _row_blocks(n_rows, body):
    for r in range(0, n_rows, BF16_ROWS):
        body(r)


def _slab_loop(body):
    def step(j, carry):
        body(j, pl.ds(pl.multiple_of(j * LANES, LANES), LANES))
        return carry
    lax.fori_loop(0, N_SLABS, step, 0)


def _lane_slab(j):
    return slice(j * LANES, (j + 1) * LANES)


def _dot(a, b):
    return jnp.dot(a, b, preferred_element_type=F32)


def _from_slabs(buf, rs):
    return jnp.concatenate([buf[j, rs, :] for j in range(N_SLABS)], axis=-1)


def _zero_after(v):
    bits = pltpu.bitcast(v, jnp.uint32)
    bits = lax.shift_right_logical(lax.shift_right_logical(bits, jnp.uint32(16)), jnp.uint32(16))
    return pltpu.bitcast(bits, F32)


def _conv_taps(load, store, w_ref, lanes, n_taps, off, rows, init, post=None, after=None):
    half = rows // 2
    acc = after
    for par in range(2):
        acc = init(half) if acc is None else init(half) + _zero_after(acc)
        for k in range(n_taps):
            acc = acc + load(par + k + off, half) * w_ref[k:k + 1, lanes]
        store(par, half, acc if post is None else post(par, half, acc))
    return acc


def _stage_rmsnorm_bf16(src_ref, g_ref, dst_ref, rows):
    def body(r):
        rs = pl.ds(r, BF16_ROWS)
        dst_ref[rs, :] = _rmsnorm(src_ref[rs, :], g_ref[...]).astype(BF16)
    _row_blocks(rows, body)


def _stage_layernorm_silu_bf16(c_buf, g_ref, b_ref, dst_ref, rows):
    def body(r):
        rs = pl.ds(r, BF16_ROWS)
        dst_ref[rs, :] = _layernorm_silu(_from_slabs(c_buf, rs), g_ref[...], b_ref[...]).astype(BF16)
    _row_blocks(rows, body)


def _stage_gate_b_bf16(p_ref, c_buf, dst_ref, rows):
    def body(r):
        rs = pl.ds(r, BF16_ROWS)
        dst_ref[rs, :] = (p_ref[rs, 0:D] * _from_slabs(c_buf, rs)).astype(BF16)
    _row_blocks(rows, body)


def _stage_merge_bf16(g_ref, ya_ref, yb_ref, dst_ref, rows):
    def body(r):
        rs = pl.ds(r, BF16_ROWS)
        m = _sigmoid(g_ref[rs, 0:D]) * ya_ref[rs, :] + _sigmoid(g_ref[rs, D:2 * D]) * yb_ref[rs, :]
        dst_ref[rs, :] = m.astype(BF16)
    _row_blocks(rows, body)


def _mixer_kernel(x_ref, ha0_ref, hb0_ref, n1g_ref, caw_ref, cab_ref, lng_ref, lnb_ref, cbw_ref,
                  w_in_ref, w_a_ref, w_b_ref, w_o_ref,
                  h_ref, na_ref, nb_ref,
                  ua_buf, zb_buf, bb_buf, c_buf, xn_buf, lhs_a, lhs_b, lhs_m, ga_buf, gb_buf, yb_buf):
    t = pl.program_id(1)
    last_t = pl.num_programs(1) - 1

    @pl.when(t == 0)
    def _():
        for j in range(N_SLABS):
            ua_buf[j, 0:HA, :] = ha0_ref[:, _lane_slab(j)]
            zb_buf[j, 0:HB, :] = hb0_ref[:, _lane_slab(j)]

    for r in range(0, TM, BF16_ROWS):
        xn_buf[r:r + BF16_ROWS, :] = _rmsnorm(x_ref[r:r + BF16_ROWS, :], n1g_ref[...]).astype(BF16)

    for c in range(N_CHUNKS):
        a_val = _dot(xn_buf[...], w_in_ref[:, COL_A_VAL + c * MXU_COLS:COL_A_VAL + (c + 1) * MXU_COLS])
        a_gate = _dot(xn_buf[...], w_in_ref[:, COL_A_GATE + c * MXU_COLS:COL_A_GATE + (c + 1) * MXU_COLS])
        u = a_val * _sigmoid(a_gate)
        for jj in range(SLABS_PER_CHUNK):
            ua_buf[c * SLABS_PER_CHUNK + jj, HA:HA + TM, :] = u[:, _lane_slab(jj)]

    def proj(col0, i):
        return _dot(xn_buf[...], w_in_ref[:, pl.ds(pl.multiple_of(col0 + i * MXU_COLS, MXU_COLS), MXU_COLS)])

    def conv_a_step(i, carry):
        cols = pl.ds(pl.multiple_of(i * MXU_COLS, MXU_COLS), MXU_COLS)
        rhs = jnp.concatenate(
            [w_in_ref[:, pl.ds(pl.multiple_of(col0 + i * MXU_COLS, MXU_COLS), MXU_COLS)]
             for col0 in (COL_B_C, COL_B_X, COL_G_A, COL_G_B)], axis=1)
        r = _dot(xn_buf[...], rhs)
        z = r[:, 0:MXU_COLS] * r[:, MXU_COLS:2 * MXU_COLS]
        for jj in range(SLABS_PER_CHUNK):
            zb_buf[i * SLABS_PER_CHUNK + jj, HB:HB + TM, :] = z[:, _lane_slab(jj)]
        ga_buf[:, cols] = r[:, 2 * MXU_COLS:3 * MXU_COLS]
        gb_buf[:, cols] = r[:, 3 * MXU_COLS:4 * MXU_COLS]
        acc = None
        for jj in range(SLABS_PER_CHUNK):
            j = i * SLABS_PER_CHUNK + jj
            lanes = pl.ds(pl.multiple_of(j * LANES, LANES), LANES)
            for r0 in range(0, TM, CONV_ROWS):
                def load(start, n, r0=r0, j=j):
                    return ua_buf[j, pl.ds(r0 + start, n, stride=2), :]
                def store(start, n, v, r0=r0, j=j):
                    c_buf[j, pl.ds(r0 + start, n, stride=2), :] = v
                acc = _conv_taps(load, store, caw_ref, lanes, KA, OFF_A, CONV_ROWS,
                                 lambda n, lanes=lanes: jnp.broadcast_to(cab_ref[:, lanes], (n, LANES)),
                                 after=acc)
        return carry
    lax.fori_loop(0, N_CHUNKS, conv_a_step, 0)


    for i in range(N_CHUNKS):
        b_b = proj(COL_B_B, i)
        for jj in range(SLABS_PER_CHUNK):
            bb_buf[i * SLABS_PER_CHUNK + jj, :, :] = b_b[:, _lane_slab(jj)]
        for jj in range(SLABS_PER_CHUNK):
            j = i * SLABS_PER_CHUNK + jj
            for r0 in range(0, TM, CONV_B_ROWS):
                def load(start, n, r0=r0, j=j):
                    return zb_buf[j, pl.ds(r0 + start, n, stride=2), :]
                def gate(par, n, acc, r0=r0, j=j):
                    return acc * bb_buf[j, pl.ds(r0 + par, n, stride=2), :]
                def store(start, n, v, r0=r0, j=j):
                    bb_buf[j, pl.ds(r0 + start, n, stride=2), :] = v
                _conv_taps(load, store, cbw_ref, _lane_slab(j), KB, OFF_B, CONV_B_ROWS,
                           lambda n: jnp.zeros((n, LANES), F32), post=gate)

    for r in range(0, TM, BF16_ROWS):
        lhs_b[r:r + BF16_ROWS, :] = _from_slabs(bb_buf, slice(r, r + BF16_ROWS)).astype(BF16)

    rows_per_chunk = TM // N_CHUNKS
    for c in range(N_CHUNKS):
        cols = slice(c * MXU_COLS, (c + 1) * MXU_COLS)
        yb_buf[:, cols] = _dot(lhs_b[...], w_b_ref[:, cols])
        for r in range(c * rows_per_chunk, (c + 1) * rows_per_chunk, BF16_ROWS):
            rs = slice(r, r + BF16_ROWS)
            lhs_a[rs, :] = _layernorm_silu(_from_slabs(c_buf, rs), lng_ref[...], lnb_ref[...]).astype(BF16)

    for c in range(N_CHUNKS):
        cols = slice(c * MXU_COLS, (c + 1) * MXU_COLS)
        y_a = _dot(lhs_a[...], w_a_ref[:, cols])
        lhs_m[:, cols] = (_sigmoid(ga_buf[:, cols]) * y_a
                          + _sigmoid(gb_buf[:, cols]) * yb_buf[:, cols]).astype(BF16)

    h_ref[...] = x_ref[...] + _dot(lhs_m[...], w_o_ref[...])

    @pl.when(t == last_t)
    def _():
        for j in range(N_SLABS):
            na_ref[:, _lane_slab(j)] = ua_buf[j, TM + OFF_A:TM + HA, :]
            nb_ref[:, _lane_slab(j)] = zb_buf[j, TM + OFF_B:TM + HB, :]
    ua_buf[:, 0:HA, :] = ua_buf[:, TM:TM + HA, :]
    zb_buf[:, 0:HB, :] = zb_buf[:, TM:TM + HB, :]


def _vmem_spec():
    return pl.BlockSpec(memory_space=pltpu.VMEM)


def _layer_spec(stacked, layer):
    zeros = (0,) * (stacked.ndim - 1)
    return pl.BlockSpec((None,) + stacked.shape[1:], lambda *_: (layer,) + zeros,
                        pipeline_mode=pl.Buffered(1))


MIXER_VECTORS = ("norm1_g", "conv_a_w", "conv_a_b", "ln_a_g", "ln_a_b", "conv_b_w")
MIXER_WEIGHTS = ("w_in", "w_a_out", "w_b_out", "w_o")
FFN_WEIGHTS = ("w_ffn_gate", "w_ffn_up", "w_ffn_down")


def _prompt_mixer(x, ha0, hb0, params, layer, bf16_weights):
    b, s, _ = x.shape
    row_spec = pl.BlockSpec((None, TM, D), lambda i, j: (i, j, 0))
    vectors = [params[k] for k in MIXER_VECTORS]
    weights = [bf16_weights[k] for k in MIXER_WEIGHTS]
    return pl.pallas_call(
        _mixer_kernel,
        grid=(b, s // TM),
        in_specs=([row_spec, _vmem_spec(), _vmem_spec()] + [_layer_spec(v, layer) for v in vectors]
                  + [_whole_spec(w.shape, single=True) for w in weights]),
        out_specs=[row_spec,
                   pl.BlockSpec((None, KA - 1, D), lambda i, j: (i, 0, 0)),
                   pl.BlockSpec((None, KB - 1, D), lambda i, j: (i, 0, 0))],
        out_shape=[jax.ShapeDtypeStruct((b, s, D), F32),
                   jax.ShapeDtypeStruct((b, KA - 1, D), F32),
                   jax.ShapeDtypeStruct((b, KB - 1, D), F32)],
        scratch_shapes=[pltpu.VMEM((N_SLABS, HA + TM, LANES), F32),
                        pltpu.VMEM((N_SLABS, HB + TM, LANES), F32),
                        pltpu.VMEM((N_SLABS, TM, LANES), F32),
                        pltpu.VMEM((N_SLABS, TM, LANES), F32),
                        pltpu.VMEM((TM, D), BF16),
                        pltpu.VMEM((TM, D), BF16),
                        pltpu.VMEM((TM, D), BF16),
                        pltpu.VMEM((TM, D), BF16),
                        pltpu.VMEM((TM, D), F32),
                        pltpu.VMEM((TM, D), F32),
                        pltpu.VMEM((TM, D), F32)],
        compiler_params=pltpu.CompilerParams(
            dimension_semantics=("arbitrary", "arbitrary"), vmem_limit_bytes=VMEM_LIMIT),
        name="prompt_mixer",
    )(x, ha0, hb0, *vectors, *weights)


def _small_ffn_kernel(final_norm, h_ref, n2g_ref, wg_f32, wu_f32, wd_f32, fg_ref,
                      out_ref, wg_out, wu_out, wd_out, hn_buf, f_buf, wg_ref, wu_ref, wd_ref):
    step = pl.program_id(0)

    @pl.when(step < CAST_STEPS)
    def _():
        _cast_chunk(step, (wg_f32, wu_f32, wd_f32), (wg_out, wu_out, wd_out), (wg_ref, wu_ref, wd_ref))

    @pl.when(step == CAST_STEPS)
    def _():
        _ffn_kernel(final_norm, h_ref, n2g_ref, wg_ref, wu_ref, wd_ref, fg_ref, out_ref, hn_buf, f_buf)


def _ffn_kernel(final_norm, h_ref, n2g_ref, wg_ref, wu_ref, wd_ref, fg_ref, out_ref, hn_buf, f_buf):
    rows = h_ref.shape[0]
    for r in range(0, rows, BF16_ROWS):
        hn_buf[r:r + BF16_ROWS, :] = _rmsnorm(h_ref[r:r + BF16_ROWS, :], n2g_ref[...]).astype(BF16)
    for c in range(0, D_HID, MXU_COLS):
        g = _dot(hn_buf[...], wg_ref[:, c:c + MXU_COLS])
        u = _dot(hn_buf[...], wu_ref[:, c:c + MXU_COLS])
        f_buf[:, c:c + MXU_COLS] = (_silu(g) * u).astype(BF16)
    out_ref[...] = h_ref[...] + _dot(f_buf[...], wd_ref[...])
    if final_norm:
        for r in range(0, rows, BF16_ROWS):
            out_ref[r:r + BF16_ROWS, :] = _rmsnorm(out_ref[r:r + BF16_ROWS, :], fg_ref[...])


def _prompt_ffn_kernel(final_norm, n_cast, h_ref, n2g_ref, wg_ref, wu_ref, wd_ref, fg_ref, *rest):
    f32_refs, out_ref, cast_outs = rest[:n_cast], rest[n_cast], rest[n_cast + 1:2 * n_cast + 1]
    hn_buf, f_buf = rest[2 * n_cast + 1:]
    if n_cast:
        @pl.when(pl.program_id(0) < CAST_STEPS)
        def _():
            for f, o in zip(f32_refs, cast_outs):
                o[...] = f[...].astype(BF16)
    _ffn_kernel(final_norm, h_ref, n2g_ref, wg_ref, wu_ref, wd_ref, fg_ref, out_ref, hn_buf, f_buf)


def _prompt_ffn(h, params, layer, final_norm, bf16_weights, cast_layer=None):
    n = h.shape[0]
    assert n // TM >= CAST_STEPS
    row_spec = pl.BlockSpec((TM, D), lambda i: (i, 0))
    weights = [bf16_weights[k] for k in FFN_WEIGHTS]
    cast_names = () if cast_layer is None else MIXER_WEIGHTS + FFN_WEIGHTS
    cast_ins = [params[k] for k in cast_names]
    cast_outs = [_cast_out(w) for w in cast_ins]
    outs = pl.pallas_call(
        functools.partial(_prompt_ffn_kernel, final_norm, len(cast_names)),
        grid=(n // TM,),
        in_specs=([row_spec, _layer_spec(params["norm2_g"], layer)]
                  + [_whole_spec(w.shape, single=True) for w in weights] + [_vmem_spec()]
                  + [_cast_in_spec(w, cast_layer) for w in cast_ins]),
        out_specs=[row_spec] + [spec for _, spec in cast_outs],
        out_shape=[jax.ShapeDtypeStruct((n, D), F32)] + [shape for shape, _ in cast_outs],
        scratch_shapes=[pltpu.VMEM((TM, D), BF16),
                        pltpu.VMEM((TM, D_HID), BF16)],
        compiler_params=pltpu.CompilerParams(
            dimension_semantics=("arbitrary",), vmem_limit_bytes=VMEM_LIMIT),
        name="prompt_ffn",
    )(h, params["norm2_g"], *weights, params["final_norm_g"], *cast_ins)
    return outs[0], dict(zip(cast_names, outs[1:]))


def _small_ffn(h, params, layer, final_norm, bf16_weights=None):
    rows = h.shape[0]
    if bf16_weights is not None:
        ready = [bf16_weights[k] for k in FFN_WEIGHTS]
        fg = params["final_norm_g"]
        out = pl.pallas_call(
            functools.partial(_ffn_kernel, final_norm),
            grid=(1,),
            in_specs=([_whole_spec(h.shape, single=True), _layer_spec(params["norm2_g"], layer)]
                      + [_whole_spec(w.shape, single=True) for w in ready] + [_whole_spec(fg.shape, single=True)]),
            out_specs=_whole_spec(h.shape),
            out_shape=jax.ShapeDtypeStruct((rows, D), F32),
            scratch_shapes=[pltpu.VMEM((rows, D), BF16), pltpu.VMEM((rows, D_HID), BF16)],
            compiler_params=pltpu.CompilerParams(
                dimension_semantics=("arbitrary",), vmem_limit_bytes=VMEM_LIMIT),
            name="small_ffn",
        )(h, params["norm2_g"], *ready, fg)
        return out, bf16_weights
    weights = [params[k] for k in FFN_WEIGHTS]
    cast_outs = [_cast_out(w) for w in weights]
    outs = pl.pallas_call(
        functools.partial(_small_ffn_kernel, final_norm),
        grid=(CAST_STEPS + 1,),
        in_specs=([_whole_spec(h.shape, single=True), _layer_spec(params["norm2_g"], layer)]
                  + [_cast_in_spec(w, layer) for w in weights]
                  + [_whole_spec(params["final_norm_g"].shape, single=True)]),
        out_specs=[_whole_spec(h.shape)] + [spec for _, spec in cast_outs],
        out_shape=[jax.ShapeDtypeStruct((rows, D), F32)] + [shape for shape, _ in cast_outs],
        scratch_shapes=[pltpu.VMEM((rows, D), BF16),
                        pltpu.VMEM((rows, D_HID), BF16)]
                       + [pltpu.VMEM(w.shape[1:], BF16) for w in weights],
        compiler_params=pltpu.CompilerParams(
            dimension_semantics=("arbitrary",), vmem_limit_bytes=VMEM_LIMIT),
        name="small_ffn",
    )(h, params["norm2_g"], *weights, params["final_norm_g"])
    return outs[0], dict(zip(FFN_WEIGHTS, outs[1:]))


def _cast_chunk(step, f32_refs, out_refs, resident_refs):
    for f, o, r in zip(f32_refs, out_refs, resident_refs):
        k = f.shape[0]
        v = f[...].astype(BF16)
        o[...] = v
        r[pl.ds(pl.multiple_of(step * k, k), k), :] = v


def _small_kernel(n_seq, seq_len,
                  x_ref, ha_ref, hb_ref, n1g_ref, caw_ref, cab_ref, lng_ref, lnb_ref, cbw_ref,
                  w_in_f32, w_a_f32, w_b_f32, w_o_f32,
                  h_ref, na_ref, nb_ref, w_in_out, w_a_out, w_b_out, w_o_out,
                  ua_buf, zb_buf, c_buf, xn_buf, lhs_buf, p_buf, ya_buf, yb_buf,
                  w_in_ref, w_a_ref, w_b_ref, w_o_ref):
    step = pl.program_id(0)

    @pl.when(step < CAST_STEPS)
    def _():
        _cast_chunk(step, (w_in_f32, w_a_f32, w_b_f32, w_o_f32), (w_in_out, w_a_out, w_b_out, w_o_out),
                    (w_in_ref, w_a_ref, w_b_ref, w_o_ref))

    @pl.when(step == CAST_STEPS)
    def _():
        _small_mixer_body(n_seq, seq_len, x_ref, ha_ref, hb_ref, n1g_ref, caw_ref, cab_ref, lng_ref,
                          lnb_ref, cbw_ref, w_in_ref, w_a_ref, w_b_ref, w_o_ref, h_ref, na_ref, nb_ref,
                          ua_buf, zb_buf, c_buf, xn_buf, lhs_buf, p_buf, ya_buf, yb_buf)


def _small_mixer_body(n_seq, seq_len,
                      x_ref, ha_ref, hb_ref, n1g_ref, caw_ref, cab_ref, lng_ref, lnb_ref, cbw_ref,
                      w_in_ref, w_a_ref, w_b_ref, w_o_ref,
                      h_ref, na_ref, nb_ref,
                      ua_buf, zb_buf, c_buf, xn_buf, lhs_buf, p_buf, ya_buf, yb_buf):
    rows = n_seq * seq_len
    for j in range(N_SLABS):
        ua_buf[:, j, 0:HA, :] = ha_ref[:, :, _lane_slab(j)]
        zb_buf[:, j, 0:HB, :] = hb_ref[:, :, _lane_slab(j)]

    _stage_rmsnorm_bf16(x_ref, n1g_ref, xn_buf, rows)

    p_buf[:, 0:2 * D] = _dot(xn_buf[...], w_in_ref[:, COL_A_VAL:COL_B_B])
    for s in range(n_seq):
        rs = slice(s * seq_len, (s + 1) * seq_len)
        u = p_buf[rs, 0:D] * _sigmoid(p_buf[rs, D:2 * D])
        for j in range(N_SLABS):
            ua_buf[s, j, HA:HA + seq_len, :] = u[:, _lane_slab(j)]

    def conv_a(j, lanes):
        for s in range(n_seq):
            def load(start, n, s=s):
                return ua_buf[s, j, pl.ds(start, n, stride=2), :]
            def store(start, n, v, s=s):
                c_buf[j, pl.ds(s * seq_len + start, n, stride=2), :] = v
            _conv_taps(load, store, caw_ref, lanes, KA, OFF_A, seq_len,
                       lambda n: jnp.broadcast_to(cab_ref[:, lanes], (n, LANES)))
    _slab_loop(conv_a)
    for j in range(N_SLABS):
        na_ref[:, :, _lane_slab(j)] = ua_buf[:, j, seq_len + OFF_A:seq_len + HA, :]

    _stage_layernorm_silu_bf16(c_buf, lng_ref, lnb_ref, lhs_buf, rows)
    ya_buf[...] = _dot(lhs_buf[...], w_a_ref[...])

    p_buf[...] = _dot(xn_buf[...], w_in_ref[:, COL_B_B:COL_G_A])
    for s in range(n_seq):
        rs = slice(s * seq_len, (s + 1) * seq_len)
        z = p_buf[rs, D:2 * D] * p_buf[rs, 2 * D:3 * D]
        for j in range(N_SLABS):
            zb_buf[s, j, HB:HB + seq_len, :] = z[:, _lane_slab(j)]

    def conv_b(j, lanes):
        for s in range(n_seq):
            def load(start, n, s=s):
                return zb_buf[s, j, pl.ds(start, n, stride=2), :]
            def store(start, n, v, s=s):
                c_buf[j, pl.ds(s * seq_len + start, n, stride=2), :] = v
            _conv_taps(load, store, cbw_ref, lanes, KB, OFF_B, seq_len,
                       lambda n: jnp.zeros((n, LANES), F32))
    _slab_loop(conv_b)
    for j in range(N_SLABS):
        nb_ref[:, :, _lane_slab(j)] = zb_buf[:, j, seq_len + OFF_B:seq_len + HB, :]

    _stage_gate_b_bf16(p_buf, c_buf, lhs_buf, rows)
    yb_buf[...] = _dot(lhs_buf[...], w_b_ref[...])

    p_buf[:, 0:2 * D] = _dot(xn_buf[...], w_in_ref[:, COL_G_A:N_IN])
    _stage_merge_bf16(p_buf, ya_buf, yb_buf, lhs_buf, rows)
    h_ref[...] = x_ref[...] + _dot(lhs_buf[...], w_o_ref[...])


def _whole_spec(shape, single=False):
    zeros = (0,) * len(shape)
    if single:
        return pl.BlockSpec(shape, lambda *_: zeros, pipeline_mode=pl.Buffered(1))
    return pl.BlockSpec(shape, lambda *_: zeros)


def _cast_in_spec(stacked, layer):
    _, k, n = stacked.shape
    return pl.BlockSpec((None, k // CAST_STEPS, n), lambda s: (layer, jnp.minimum(s, CAST_STEPS - 1), 0))


def _cast_out(stacked):
    _, k, n = stacked.shape
    return (jax.ShapeDtypeStruct((k, n), BF16),
            pl.BlockSpec((k // CAST_STEPS, n), lambda s: (jnp.minimum(s, CAST_STEPS - 1), 0)))


def _small_mixer(x, ha, hb, params, layer, bf16_weights=None):
    n_seq = ha.shape[0]
    rows = x.shape[0]
    seq_len = rows // n_seq
    vectors = [params[k] for k in MIXER_VECTORS]
    out_shape = [jax.ShapeDtypeStruct((rows, D), F32),
                 jax.ShapeDtypeStruct((n_seq, KA - 1, D), F32),
                 jax.ShapeDtypeStruct((n_seq, KB - 1, D), F32)]
    data_specs = ([_whole_spec(a.shape, single=True) for a in (x, ha, hb)]
                  + [_layer_spec(v, layer) for v in vectors])
    scratch = [pltpu.VMEM((n_seq, N_SLABS, HA + seq_len, LANES), F32),
               pltpu.VMEM((n_seq, N_SLABS, HB + seq_len, LANES), F32),
               pltpu.VMEM((N_SLABS, rows, LANES), F32),
               pltpu.VMEM((rows, D), BF16),
               pltpu.VMEM((rows, D), BF16),
               pltpu.VMEM((rows, 3 * D), F32),
               pltpu.VMEM((rows, D), F32),
               pltpu.VMEM((rows, D), F32)]
    compiler_params = pltpu.CompilerParams(dimension_semantics=("arbitrary",), vmem_limit_bytes=VMEM_LIMIT)
    if bf16_weights is not None:
        ready = [bf16_weights[k] for k in MIXER_WEIGHTS]
        outs = pl.pallas_call(
            functools.partial(_small_mixer_body, n_seq, seq_len),
            grid=(1,),
            in_specs=data_specs + [_whole_spec(w.shape, single=True) for w in ready],
            out_specs=[_whole_spec(o.shape) for o in out_shape],
            out_shape=out_shape,
            scratch_shapes=scratch,
            compiler_params=compiler_params,
            name="small_mixer",
        )(x, ha, hb, *vectors, *ready)
        return outs, bf16_weights
    weights = [params[k] for k in MIXER_WEIGHTS]
    cast_outs = [_cast_out(w) for w in weights]
    outs = pl.pallas_call(
        functools.partial(_small_kernel, n_seq, seq_len),
        grid=(CAST_STEPS + 1,),
        in_specs=data_specs + [_cast_in_spec(w, layer) for w in weights],
        out_specs=[_whole_spec(o.shape) for o in out_shape] + [spec for _, spec in cast_outs],
        out_shape=out_shape + [shape for shape, _ in cast_outs],
        scratch_shapes=scratch + [pltpu.VMEM(w.shape[1:], BF16) for w in weights],
        compiler_params=compiler_params,
        name="small_mixer",
    )(x, ha, hb, *vectors, *weights)
    return outs[:3], dict(zip(MIXER_WEIGHTS, outs[3:]))


def _front_pad(a, n_rows):
    pad = [(0, 0)] * a.ndim
    pad[-2] = (n_rows - a.shape[-2], 0)
    return jnp.pad(a, pad)


def kernel(x_prompt, x_sample, state_conv_a, state_conv_b, meta_tokens, norm1_g, w_in, conv_a_w, conv_a_b, ln_a_g, ln_a_b, w_a_out, conv_b_w, w_b_out, w_o, norm2_g, w_ffn_gate, w_ffn_up, w_ffn_down, final_norm_g):
    depth = w_in.shape[0]
    b_p, seq, _ = x_prompt.shape
    b_s, seq_s, _ = x_sample.shape
    assert seq % TM == 0 and seq_s == N_META and seq_s % BF16_ROWS == 0

    rows_of = lambda v: v.reshape(depth, 1, -1).astype(F32)
    params = dict(
        norm1_g=rows_of(norm1_g), conv_a_w=conv_a_w, conv_a_b=rows_of(conv_a_b),
        ln_a_g=rows_of(ln_a_g), ln_a_b=rows_of(ln_a_b), conv_b_w=conv_b_w,
        w_in=w_in, w_a_out=w_a_out, w_b_out=w_b_out, w_o=w_o, norm2_g=rows_of(norm2_g),
        w_ffn_gate=w_ffn_gate, w_ffn_up=w_ffn_up, w_ffn_down=w_ffn_down,
        final_norm_g=final_norm_g.reshape(1, -1).astype(F32))

    xs = jnp.concatenate([meta_tokens.astype(F32), x_sample.reshape(b_s * seq_s, D)], axis=0)
    xp = x_prompt
    pa, pb, sa, sb = [], [], [], []
    layer_bf16 = None
    for l in range(depth):
        last = l == depth - 1
        ha = _front_pad(jnp.concatenate([jnp.zeros((1, KA - 1, D), F32), state_conv_a[l]], axis=0), HA)
        hb = _front_pad(jnp.concatenate([jnp.zeros((1, KB - 1, D), F32), state_conv_b[l]], axis=0), HB)
        (hs, na_s, nb_s), mixer_bf16 = _small_mixer(xs, ha, hb, params, l, layer_bf16)
        xs, ffn_bf16 = _small_ffn(hs, params, l, last, layer_bf16)
        sa.append(na_s[1:])
        sb.append(nb_s[1:])
        h, na_p, nb_p = _prompt_mixer(xp, _front_pad(na_s[0], HA), _front_pad(nb_s[0], HB), params, l,
                                      mixer_bf16)
        pa.append(na_p)
        pb.append(nb_p)
        xp, layer_bf16 = _prompt_ffn(h.reshape(b_p * seq, D), params, l, last, ffn_bf16,
                                     cast_layer=None if last else l + 1)
        xp = xp.reshape(b_p, seq, D)
        layer_bf16 = layer_bf16 or None

    y_prompt = xp
    y_sample = xs[N_META:].reshape(b_s, seq_s, D)
    return (y_prompt, y_sample, jnp.stack(pa, axis=0), jnp.stack(pb, axis=0),
            jnp.stack(sa, axis=0), jnp.stack(sb, axis=0))
```

```python
import functools

import jax
import jax.numpy as jnp
from jax import lax
from jax.experimental import pallas as pl
from jax.experimental.pallas import tpu as pltpu

F32 = jnp.float32
BF16 = jnp.bfloat16

D = 1024
D_HID = 2816
KA = 31
KB = 3
N_META = 16
RMS_EPS = 1e-6
LN_EPS = 1e-5

LANES = 128
BF16_ROWS = 16
N_SLABS = D // LANES
MXU_COLS = 256
SLABS_PER_CHUNK = MXU_COLS // LANES
N_CHUNKS = D // MXU_COLS

HA = 32
HB = 8
OFF_A = HA - (KA - 1)
OFF_B = HB - (KB - 1)
TM = 512
CONV_ROWS = 64
CONV_B_ROWS = 128
CAST_STEPS = 16
VMEM_LIMIT = 60000 * 1024

COL_A_VAL, COL_A_GATE, COL_B_B, COL_B_C, COL_B_X, COL_G_A, COL_G_B = (i * D for i in range(7))
N_IN = 7 * D


def _sigmoid(x):
    return 0.5 * jnp.tanh(0.5 * x) + 0.5


def _silu(x):
    h = 0.5 * x
    return h * jnp.tanh(h) + h


def _rmsnorm(x, g):
    return x * lax.rsqrt(jnp.mean(x * x, axis=-1, keepdims=True) + RMS_EPS) * g


def _layernorm_silu(c, g, b):
    mu = jnp.mean(c, axis=-1, keepdims=True)
    d = c - mu
    var = jnp.mean(d * d, axis=-1, keepdims=True)
    h = d * lax.rsqrt(var + LN_EPS) * (0.5 * g) + 0.5 * b
    return h * jnp.tanh(h) + h


def _row_blocks(n_rows, body):
    for r in range(0, n_rows, BF16_ROWS):
        body(r)


def _slab_loop(body):
    def step(j, carry):
        body(j, pl.ds(pl.multiple_of(j * LANES, LANES), LANES))
        return carry
    lax.fori_loop(0, N_SLABS, step, 0)


def _lane_slab(j):
    return slice(j * LANES, (j + 1) * LANES)


def _dot(a, b):
    return jnp.dot(a, b, preferred_element_type=F32)


def _from_slabs(buf, rs):
    return jnp.concatenate([buf[j, rs, :] for j in range(N_SLABS)], axis=-1)


def _zero_after(v):
    bits = pltpu.bitcast(v, jnp.uint32)
    bits = lax.shift_right_logical(lax.shift_right_logical(bits, jnp.uint32(16)), jnp.uint32(16))
    return pltpu.bitcast(bits, F32)


def _conv_taps(load, store, w_ref, lanes, n_taps, off, rows, init, post=None, after=None):
    half = rows // 2
    acc = after
    for par in range(2):
        acc = init(half) if acc is None else init(half) + _zero_after(acc)
        for k in range(n_taps):
            acc = acc + load(par + k + off, half) * w_ref[k:k + 1, lanes]
        store(par, half, acc if post is None else post(par, half, acc))
    return acc


def _stage_rmsnorm_bf16(src_ref, g_ref, dst_ref, rows):
    def body(r):
        rs = pl.ds(r, BF16_ROWS)
        dst_ref[rs, :] = _rmsnorm(src_ref[rs, :], g_ref[...]).astype(BF16)
    _row_blocks(rows, body)


def _stage_layernorm_silu_bf16(c_buf, g_ref, b_ref, dst_ref, rows):
    def body(r):
        rs = pl.ds(r, BF16_ROWS)
        dst_ref[rs, :] = _layernorm_silu(_from_slabs(c_buf, rs), g_ref[...], b_ref[...]).astype(BF16)
    _row_blocks(rows, body)


def _stage_gate_b_bf16(p_ref, c_buf, dst_ref, rows):
    def body(r):
        rs = pl.ds(r, BF16_ROWS)
        dst_ref[rs, :] = (p_ref[rs, 0:D] * _from_slabs(c_buf, rs)).astype(BF16)
    _row_blocks(rows, body)


def _stage_merge_bf16(g_ref, ya_ref, yb_ref, dst_ref, rows):
    def body(r):
        rs = pl.ds(r, BF16_ROWS)
        m = _sigmoid(g_ref[rs, 0:D]) * ya_ref[rs, :] + _sigmoid(g_ref[rs, D:2 * D]) * yb_ref[rs, :]
        dst_ref[rs, :] = m.astype(BF16)
    _row_blocks(rows, body)


def _mixer_kernel(x_ref, ha0_ref, hb0_ref, n1g_ref, caw_ref, cab_ref, lng_ref, lnb_ref, cbw_ref,
                  w_in_ref, w_a_ref, w_b_ref, w_o_ref,
                  h_ref, na_ref, nb_ref,
                  ua_buf, zb_buf, bb_buf, c_buf, xn_buf, lhs_a, lhs_b, lhs_m, ga_buf, gb_buf, yb_buf):
    t = pl.program_id(1)
    last_t = pl.num_programs(1) - 1

    @pl.when(t == 0)
    def _():
        for j in range(N_SLABS):
            ua_buf[j, 0:HA, :] = ha0_ref[:, _lane_slab(j)]
            zb_buf[j, 0:HB, :] = hb0_ref[:, _lane_slab(j)]

    for r in range(0, TM, BF16_ROWS):
        xn_buf[r:r + BF16_ROWS, :] = _rmsnorm(x_ref[r:r + BF16_ROWS, :], n1g_ref[...]).astype(BF16)

    for c in range(N_CHUNKS):
        a_val = _dot(xn_buf[...], w_in_ref[:, COL_A_VAL + c * MXU_COLS:COL_A_VAL + (c + 1) * MXU_COLS])
        a_gate = _dot(xn_buf[...], w_in_ref[:, COL_A_GATE + c * MXU_COLS:COL_A_GATE + (c + 1) * MXU_COLS])
        u = a_val * _sigmoid(a_gate)
        for jj in range(SLABS_PER_CHUNK):
            ua_buf[c * SLABS_PER_CHUNK + jj, HA:HA + TM, :] = u[:, _lane_slab(jj)]

    def proj(col0, i):
        return _dot(xn_buf[...], w_in_ref[:, pl.ds(pl.multiple_of(col0 + i * MXU_COLS, MXU_COLS), MXU_COLS)])

    def conv_a_step(i, carry):
        cols = pl.ds(pl.multiple_of(i * MXU_COLS, MXU_COLS), MXU_COLS)
        rhs = jnp.concatenate(
            [w_in_ref[:, pl.ds(pl.multiple_of(col0 + i * MXU_COLS, MXU_COLS), MXU_COLS)]
             for col0 in (COL_B_C, COL_B_X, COL_G_A, COL_G_B)], axis=1)
        r = _dot(xn_buf[...], rhs)
        z = r[:, 0:MXU_COLS] * r[:, MXU_COLS:2 * MXU_COLS]
        for jj in range(SLABS_PER_CHUNK):
            zb_buf[i * SLABS_PER_CHUNK + jj, HB:HB + TM, :] = z[:, _lane_slab(jj)]
        ga_buf[:, cols] = r[:, 2 * MXU_COLS:3 * MXU_COLS]
        gb_buf[:, cols] = r[:, 3 * MXU_COLS:4 * MXU_COLS]
        acc = None
        for jj in range(SLABS_PER_CHUNK):
            j = i * SLABS_PER_CHUNK + jj
            lanes = pl.ds(pl.multiple_of(j * LANES, LANES), LANES)
            for r0 in range(0, TM, CONV_ROWS):
                def load(start, n, r0=r0, j=j):
                    return ua_buf[j, pl.ds(r0 + start, n, stride=2), :]
                def store(start, n, v, r0=r0, j=j):
                    c_buf[j, pl.ds(r0 + start, n, stride=2), :] = v
                acc = _conv_taps(load, store, caw_ref, lanes, KA, OFF_A, CONV_ROWS,
                                 lambda n, lanes=lanes: jnp.broadcast_to(cab_ref[:, lanes], (n, LANES)),
                                 after=acc)
        return carry
    lax.fori_loop(0, N_CHUNKS, conv_a_step, 0)


    for i in range(N_CHUNKS):
        b_b = proj(COL_B_B, i)
        for jj in range(SLABS_PER_CHUNK):
            bb_buf[i * SLABS_PER_CHUNK + jj, :, :] = b_b[:, _lane_slab(jj)]
        for jj in range(SLABS_PER_CHUNK):
            j = i * SLABS_PER_CHUNK + jj
            for r0 in range(0, TM, CONV_B_ROWS):
                def load(start, n, r0=r0, j=j):
                    return zb_buf[j, pl.ds(r0 + start, n, stride=2), :]
                def gate(par, n, acc, r0=r0, j=j):
                    return acc * bb_buf[j, pl.ds(r0 + par, n, stride=2), :]
                def store(start, n, v, r0=r0, j=j):
                    bb_buf[j, pl.ds(r0 + start, n, stride=2), :] = v
                _conv_taps(load, store, cbw_ref, _lane_slab(j), KB, OFF_B, CONV_B_ROWS,
                           lambda n: jnp.zeros((n, LANES), F32), post=gate)

    for r in range(0, TM, BF16_ROWS):
        lhs_b[r:r + BF16_ROWS, :] = _from_slabs(bb_buf, slice(r, r + BF16_ROWS)).astype(BF16)

    rows_per_chunk = TM // N_CHUNKS
    for c in range(N_CHUNKS):
        cols = slice(c * MXU_COLS, (c + 1) * MXU_COLS)
        yb_buf[:, cols] = _dot(lhs_b[...], w_b_ref[:, cols])
        for r in range(c * rows_per_chunk, (c + 1) * rows_per_chunk, BF16_ROWS):
            rs = slice(r, r + BF16_ROWS)
            lhs_a[rs, :] = _layernorm_silu(_from_slabs(c_buf, rs), lng_ref[...], lnb_ref[...]).astype(BF16)

    for c in range(N_CHUNKS):
        cols = slice(c * MXU_COLS, (c + 1) * MXU_COLS)
        y_a = _dot(lhs_a[...], w_a_ref[:, cols])
        lhs_m[:, cols] = (_sigmoid(ga_buf[:, cols]) * y_a
                          + _sigmoid(gb_buf[:, cols]) * yb_buf[:, cols]).astype(BF16)

    h_ref[...] = x_ref[...] + _dot(lhs_m[...], w_o_ref[...])

    @pl.when(t == last_t)
    def _():
        for j in range(N_SLABS):
            na_ref[:, _lane_slab(j)] = ua_buf[j, TM + OFF_A:TM + HA, :]
            nb_ref[:, _lane_slab(j)] = zb_buf[j, TM + OFF_B:TM + HB, :]
    ua_buf[:, 0:HA, :] = ua_buf[:, TM:TM + HA, :]
    zb_buf[:, 0:HB, :] = zb_buf[:, TM:TM + HB, :]


def _vmem_spec():
    return pl.BlockSpec(memory_space=pltpu.VMEM)


def _layer_spec(stacked, layer):
    zeros = (0,) * (stacked.ndim - 1)
    return pl.BlockSpec((None,) + stacked.shape[1:], lambda *_: (layer,) + zeros,
                        pipeline_mode=pl.Buffered(1))


MIXER_VECTORS = ("norm1_g", "conv_a_w", "conv_a_b", "ln_a_g", "ln_a_b", "conv_b_w")
MIXER_WEIGHTS = ("w_in", "w_a_out", "w_b_out", "w_o")
FFN_WEIGHTS = ("w_ffn_gate", "w_ffn_up", "w_ffn_down")


def _prompt_mixer(x, ha0, hb0, params, layer, bf16_weights):
    b, s, _ = x.shape
    row_spec = pl.BlockSpec((None, TM, D), lambda i, j: (i, j, 0))
    vectors = [params[k] for k in MIXER_VECTORS]
    weights = [bf16_weights[k] for k in MIXER_WEIGHTS]
    return pl.pallas_call(
        _mixer_kernel,
        grid=(b, s // TM),
        in_specs=([row_spec, _vmem_spec(), _vmem_spec()] + [_layer_spec(v, layer) for v in vectors]
                  + [_whole_spec(w.shape, single=True) for w in weights]),
        out_specs=[row_spec,
                   pl.BlockSpec((None, KA - 1, D), lambda i, j: (i, 0, 0)),
                   pl.BlockSpec((None, KB - 1, D), lambda i, j: (i, 0, 0))],
        out_shape=[jax.ShapeDtypeStruct((b, s, D), F32),
                   jax.ShapeDtypeStruct((b, KA - 1, D), F32),
                   jax.ShapeDtypeStruct((b, KB - 1, D), F32)],
        scratch_shapes=[pltpu.VMEM((N_SLABS, HA + TM, LANES), F32),
                        pltpu.VMEM((N_SLABS, HB + TM, LANES), F32),
                        pltpu.VMEM((N_SLABS, TM, LANES), F32),
                        pltpu.VMEM((N_SLABS, TM, LANES), F32),
                        pltpu.VMEM((TM, D), BF16),
                        pltpu.VMEM((TM, D), BF16),
                        pltpu.VMEM((TM, D), BF16),
                        pltpu.VMEM((TM, D), BF16),
                        pltpu.VMEM((TM, D), F32),
                        pltpu.VMEM((TM, D), F32),
                        pltpu.VMEM((TM, D), F32)],
        compiler_params=pltpu.CompilerParams(
            dimension_semantics=("arbitrary", "arbitrary"), vmem_limit_bytes=VMEM_LIMIT),
        name="prompt_mixer",
    )(x, ha0, hb0, *vectors, *weights)


def _small_ffn_kernel(final_norm, h_ref, n2g_ref, wg_f32, wu_f32, wd_f32, fg_ref,
                      out_ref, wg_out, wu_out, wd_out, hn_buf, f_buf, wg_ref, wu_ref, wd_ref):
    step = pl.program_id(0)

    @pl.when(step < CAST_STEPS)
    def _():
        _cast_chunk(step, (wg_f32, wu_f32, wd_f32), (wg_out, wu_out, wd_out), (wg_ref, wu_ref, wd_ref))

    @pl.when(step == CAST_STEPS)
    def _():
        _ffn_kernel(final_norm, h_ref, n2g_ref, wg_ref, wu_ref, wd_ref, fg_ref, out_ref, hn_buf, f_buf)


def _ffn_kernel(final_norm, h_ref, n2g_ref, wg_ref, wu_ref, wd_ref, fg_ref, out_ref, hn_buf, f_buf):
    rows = h_ref.shape[0]
    for r in range(0, rows, BF16_ROWS):
        hn_buf[r:r + BF16_ROWS, :] = _rmsnorm(h_ref[r:r + BF16_ROWS, :], n2g_ref[...]).astype(BF16)
    for c in range(0, D_HID, MXU_COLS):
        g = _dot(hn_buf[...], wg_ref[:, c:c + MXU_COLS])
        u = _dot(hn_buf[...], wu_ref[:, c:c + MXU_COLS])
        f_buf[:, c:c + MXU_COLS] = (_silu(g) * u).astype(BF16)
    out_ref[...] = h_ref[...] + _dot(f_buf[...], wd_ref[...])
    if final_norm:
        for r in range(0, rows, BF16_ROWS):
            out_ref[r:r + BF16_ROWS, :] = _rmsnorm(out_ref[r:r + BF16_ROWS, :], fg_ref[...])


def _prompt_ffn_kernel(final_norm, n_cast, h_ref, n2g_ref, wg_ref, wu_ref, wd_ref, fg_ref, *rest):
    f32_refs, out_ref, cast_outs = rest[:n_cast], rest[n_cast], rest[n_cast + 1:2 * n_cast + 1]
    hn_buf, f_buf = rest[2 * n_cast + 1:]
    if n_cast:
        @pl.when(pl.program_id(0) < CAST_STEPS)
        def _():
            for f, o in zip(f32_refs, cast_outs):
                o[...] = f[...].astype(BF16)
    _ffn_kernel(final_norm, h_ref, n2g_ref, wg_ref, wu_ref, wd_ref, fg_ref, out_ref, hn_buf, f_buf)


def _prompt_ffn(h, params, layer, final_norm, bf16_weights, cast_layer=None):
    n = h.shape[0]
    assert n // TM >= CAST_STEPS
    row_spec = pl.BlockSpec((TM, D), lambda i: (i, 0))
    weights = [bf16_weights[k] for k in FFN_WEIGHTS]
    cast_names = () if cast_layer is None else MIXER_WEIGHTS + FFN_WEIGHTS
    cast_ins = [params[k] for k in cast_names]
    cast_outs = [_cast_out(w) for w in cast_ins]
    outs = pl.pallas_call(
        functools.partial(_prompt_ffn_kernel, final_norm, len(cast_names)),
        grid=(n // TM,),
        in_specs=([row_spec, _layer_spec(params["norm2_g"], layer)]
                  + [_whole_spec(w.shape, single=True) for w in weights] + [_vmem_spec()]
                  + [_cast_in_spec(w, cast_layer) for w in cast_ins]),
        out_specs=[row_spec] + [spec for _, spec in cast_outs],
        out_shape=[jax.ShapeDtypeStruct((n, D), F32)] + [shape for shape, _ in cast_outs],
        scratch_shapes=[pltpu.VMEM((TM, D), BF16),
                        pltpu.VMEM((TM, D_HID), BF16)],
        compiler_params=pltpu.CompilerParams(
            dimension_semantics=("arbitrary",), vmem_limit_bytes=VMEM_LIMIT),
        name="prompt_ffn",
    )(h, params["norm2_g"], *weights, params["final_norm_g"], *cast_ins)
    return outs[0], dict(zip(cast_names, outs[1:]))


def _small_ffn(h, params, layer, final_norm, bf16_weights=None):
    rows = h.shape[0]
    if bf16_weights is not None:
        ready = [bf16_weights[k] for k in FFN_WEIGHTS]
        fg = params["final_norm_g"]
        out = pl.pallas_call(
            functools.partial(_ffn_kernel, final_norm),
            grid=(1,),
            in_specs=([_whole_spec(h.shape, single=True), _layer_spec(params["norm2_g"], layer)]
                      + [_whole_spec(w.shape, single=True) for w in ready] + [_whole_spec(fg.shape, single=True)]),
            out_specs=_whole_spec(h.shape),
            out_shape=jax.ShapeDtypeStruct((rows, D), F32),
            scratch_shapes=[pltpu.VMEM((rows, D), BF16), pltpu.VMEM((rows, D_HID), BF16)],
            compiler_params=pltpu.CompilerParams(
                dimension_semantics=("arbitrary",), vmem_limit_bytes=VMEM_LIMIT),
            name="small_ffn",
        )(h, params["norm2_g"], *ready, fg)
        return out, bf16_weights
    weights = [params[k] for k in FFN_WEIGHTS]
    cast_outs = [_cast_out(w) for w in weights]
    outs = pl.pallas_call(
        functools.partial(_small_ffn_kernel, final_norm),
        grid=(CAST_STEPS + 1,),
        in_specs=([_whole_spec(h.shape, single=True), _layer_spec(params["norm2_g"], layer)]
                  + [_cast_in_spec(w, layer) for w in weights]
                  + [_whole_spec(params["final_norm_g"].shape, single=True)]),
        out_specs=[_whole_spec(h.shape)] + [spec for _, spec in cast_outs],
        out_shape=[jax.ShapeDtypeStruct((rows, D), F32)] + [shape for shape, _ in cast_outs],
        scratch_shapes=[pltpu.VMEM((rows, D), BF16),
                        pltpu.VMEM((rows, D_HID), BF16)]
                       + [pltpu.VMEM(w.shape[1:], BF16) for w in weights],
        compiler_params=pltpu.CompilerParams(
            dimension_semantics=("arbitrary",), vmem_limit_bytes=VMEM_LIMIT),
        name="small_ffn",
    )(h, params["norm2_g"], *weights, params["final_norm_g"])
    return outs[0], dict(zip(FFN_WEIGHTS, outs[1:]))


def _cast_chunk(step, f32_refs, out_refs, resident_refs):
    for f, o, r in zip(f32_refs, out_refs, resident_refs):
        k = f.shape[0]
        v = f[...].astype(BF16)
        o[...] = v
        r[pl.ds(pl.multiple_of(step * k, k), k), :] = v


def _small_kernel(n_seq, seq_len,
                  x_ref, ha_ref, hb_ref, n1g_ref, caw_ref, cab_ref, lng_ref, lnb_ref, cbw_ref,
                  w_in_f32, w_a_f32, w_b_f32, w_o_f32,
                  h_ref, na_ref, nb_ref, w_in_out, w_a_out, w_b_out, w_o_out,
                  ua_buf, zb_buf, c_buf, xn_buf, lhs_buf, p_buf, ya_buf, yb_buf,
                  w_in_ref, w_a_ref, w_b_ref, w_o_ref):
    step = pl.program_id(0)

    @pl.when(step < CAST_STEPS)
    def _():
        _cast_chunk(step, (w_in_f32, w_a_f32, w_b_f32, w_o_f32), (w_in_out, w_a_out, w_b_out, w_o_out),
                    (w_in_ref, w_a_ref, w_b_ref, w_o_ref))

    @pl.when(step == CAST_STEPS)
    def _():
        _small_mixer_body(n_seq, seq_len, x_ref, ha_ref, hb_ref, n1g_ref, caw_ref, cab_ref, lng_ref,
                          lnb_ref, cbw_ref, w_in_ref, w_a_ref, w_b_ref, w_o_ref, h_ref, na_ref, nb_ref,
                          ua_buf, zb_buf, c_buf, xn_buf, lhs_buf, p_buf, ya_buf, yb_buf)


def _small_mixer_body(n_seq, seq_len,
                      x_ref, ha_ref, hb_ref, n1g_ref, caw_ref, cab_ref, lng_ref, lnb_ref, cbw_ref,
                      w_in_ref, w_a_ref, w_b_ref, w_o_ref,
                      h_ref, na_ref, nb_ref,
                      ua_buf, zb_buf, c_buf, xn_buf, lhs_buf, p_buf, ya_buf, yb_buf):
    rows = n_seq * seq_len
    for j in range(N_SLABS):
        ua_buf[:, j, 0:HA, :] = ha_ref[:, :, _lane_slab(j)]
        zb_buf[:, j, 0:HB, :] = hb_ref[:, :, _lane_slab(j)]

    _stage_rmsnorm_bf16(x_ref, n1g_ref, xn_buf, rows)

    p_buf[:, 0:2 * D] = _dot(xn_buf[...], w_in_ref[:, COL_A_VAL:COL_B_B])
    for s in range(n_seq):
        rs = slice(s * seq_len, (s + 1) * seq_len)
        u = p_buf[rs, 0:D] * _sigmoid(p_buf[rs, D:2 * D])
        for j in range(N_SLABS):
            ua_buf[s, j, HA:HA + seq_len, :] = u[:, _lane_slab(j)]

    def conv_a(j, lanes):
        for s in range(n_seq):
            def load(start, n, s=s):
                return ua_buf[s, j, pl.ds(start, n, stride=2), :]
            def store(start, n, v, s=s):
                c_buf[j, pl.ds(s * seq_len + start, n, stride=2), :] = v
            _conv_taps(load, store, caw_ref, lanes, KA, OFF_A, seq_len,
                       lambda n: jnp.broadcast_to(cab_ref[:, lanes], (n, LANES)))
    _slab_loop(conv_a)
    for j in range(N_SLABS):
        na_ref[:, :, _lane_slab(j)] = ua_buf[:, j, seq_len + OFF_A:seq_len + HA, :]

    _stage_layernorm_silu_bf16(c_buf, lng_ref, lnb_ref, lhs_buf, rows)
    ya_buf[...] = _dot(lhs_buf[...], w_a_ref[...])

    p_buf[...] = _dot(xn_buf[...], w_in_ref[:, COL_B_B:COL_G_A])
    for s in range(n_seq):
        rs = slice(s * seq_len, (s + 1) * seq_len)
        z = p_buf[rs, D:2 * D] * p_buf[rs, 2 * D:3 * D]
        for j in range(N_SLABS):
            zb_buf[s, j, HB:HB + seq_len, :] = z[:, _lane_slab(j)]

    def conv_b(j, lanes):
        for s in range(n_seq):
            def load(start, n, s=s):
                return zb_buf[s, j, pl.ds(start, n, stride=2), :]
            def store(start, n, v, s=s):
                c_buf[j, pl.ds(s * seq_len + start, n, stride=2), :] = v
            _conv_taps(load, store, cbw_ref, lanes, KB, OFF_B, seq_len,
                       lambda n: jnp.zeros((n, LANES), F32))
    _slab_loop(conv_b)
    for j in range(N_SLABS):
        nb_ref[:, :, _lane_slab(j)] = zb_buf[:, j, seq_len + OFF_B:seq_len + HB, :]

    _stage_gate_b_bf16(p_buf, c_buf, lhs_buf, rows)
    yb_buf[...] = _dot(lhs_buf[...], w_b_ref[...])

    p_buf[:, 0:2 * D] = _dot(xn_buf[...], w_in_ref[:, COL_G_A:N_IN])
    _stage_merge_bf16(p_buf, ya_buf, yb_buf, lhs_buf, rows)
    h_ref[...] = x_ref[...] + _dot(lhs_buf[...], w_o_ref[...])


def _whole_spec(shape, single=False):
    zeros = (0,) * len(shape)
    if single:
        return pl.BlockSpec(shape, lambda *_: zeros, pipeline_mode=pl.Buffered(1))
    return pl.BlockSpec(shape, lambda *_: zeros)


def _cast_in_spec(stacked, layer):
    _, k, n = stacked.shape
    return pl.BlockSpec((None, k // CAST_STEPS, n), lambda s: (layer, jnp.minimum(s, CAST_STEPS - 1), 0))


def _cast_out(stacked):
    _, k, n = stacked.shape
    return (jax.ShapeDtypeStruct((k, n), BF16),
            pl.BlockSpec((k // CAST_STEPS, n), lambda s: (jnp.minimum(s, CAST_STEPS - 1), 0)))


def _small_mixer(x, ha, hb, params, layer, bf16_weights=None):
    n_seq = ha.shape[0]
    rows = x.shape[0]
    seq_len = rows // n_seq
    vectors = [params[k] for k in MIXER_VECTORS]
    out_shape = [jax.ShapeDtypeStruct((rows, D), F32),
                 jax.ShapeDtypeStruct((n_seq, KA - 1, D), F32),
                 jax.ShapeDtypeStruct((n_seq, KB - 1, D), F32)]
    data_specs = ([_whole_spec(a.shape, single=True) for a in (x, ha, hb)]
                  + [_layer_spec(v, layer) for v in vectors])
    scratch = [pltpu.VMEM((n_seq, N_SLABS, HA + seq_len, LANES), F32),
               pltpu.VMEM((n_seq, N_SLABS, HB + seq_len, LANES), F32),
               pltpu.VMEM((N_SLABS, rows, LANES), F32),
               pltpu.VMEM((rows, D), BF16),
               pltpu.VMEM((rows, D), BF16),
               pltpu.VMEM((rows, 3 * D), F32),
               pltpu.VMEM((rows, D), F32),
               pltpu.VMEM((rows, D), F32)]
    compiler_params = pltpu.CompilerParams(dimension_semantics=("arbitrary",), vmem_limit_bytes=VMEM_LIMIT)
    if bf16_weights is not None:
        ready = [bf16_weights[k] for k in MIXER_WEIGHTS]
        outs = pl.pallas_call(
            functools.partial(_small_mixer_body, n_seq, seq_len),
            grid=(1,),
            in_specs=data_specs + [_whole_spec(w.shape, single=True) for w in ready],
            out_specs=[_whole_spec(o.shape) for o in out_shape],
            out_shape=out_shape,
            scratch_shapes=scratch,
            compiler_params=compiler_params,
            name="small_mixer",
        )(x, ha, hb, *vectors, *ready)
        return outs, bf16_weights
    weights = [params[k] for k in MIXER_WEIGHTS]
    cast_outs = [_cast_out(w) for w in weights]
    outs = pl.pallas_call(
        functools.partial(_small_kernel, n_seq, seq_len),
        grid=(CAST_STEPS + 1,),
        in_specs=data_specs + [_cast_in_spec(w, layer) for w in weights],
        out_specs=[_whole_spec(o.shape) for o in out_shape] + [spec for _, spec in cast_outs],
        out_shape=out_shape + [shape for shape, _ in cast_outs],
        scratch_shapes=scratch + [pltpu.VMEM(w.shape[1:], BF16) for w in weights],
        compiler_params=compiler_params,
        name="small_mixer",
    )(x, ha, hb, *vectors, *weights)
    return outs[:3], dict(zip(MIXER_WEIGHTS, outs[3:]))


def _front_pad(a, n_rows):
    pad = [(0, 0)] * a.ndim
    pad[-2] = (n_rows - a.shape[-2], 0)
    return jnp.pad(a, pad)


def kernel(x_prompt, x_sample, state_conv_a, state_conv_b, meta_tokens, norm1_g, w_in, conv_a_w, conv_a_b, ln_a_g, ln_a_b, w_a_out, conv_b_w, w_b_out, w_o, norm2_g, w_ffn_gate, w_ffn_up, w_ffn_down, final_norm_g):
    depth = w_in.shape[0]
    b_p, seq, _ = x_prompt.shape
    b_s, seq_s, _ = x_sample.shape
    assert seq % TM == 0 and seq_s == N_META and seq_s % BF16_ROWS == 0

    rows_of = lambda v: v.reshape(depth, 1, -1).astype(F32)
    params = dict(
        norm1_g=rows_of(norm1_g), conv_a_w=conv_a_w, conv_a_b=rows_of(conv_a_b),
        ln_a_g=rows_of(ln_a_g), ln_a_b=rows_of(ln_a_b), conv_b_w=conv_b_w,
        w_in=w_in, w_a_out=w_a_out, w_b_out=w_b_out, w_o=w_o, norm2_g=rows_of(norm2_g),
        w_ffn_gate=w_ffn_gate, w_ffn_up=w_ffn_up, w_ffn_down=w_ffn_down,
        final_norm_g=final_norm_g.reshape(1, -1).astype(F32))

    xs = jnp.concatenate([meta_tokens.astype(F32), x_sample.reshape(b_s * seq_s, D)], axis=0)
    xp = x_prompt
    pa, pb, sa, sb = [], [], [], []
    layer_bf16 = None
    for l in range(depth):
        last = l == depth - 1
        ha = _front_pad(jnp.concatenate([jnp.zeros((1, KA - 1, D), F32), state_conv_a[l]], axis=0), HA)
        hb = _front_pad(jnp.concatenate([jnp.zeros((1, KB - 1, D), F32), state_conv_b[l]], axis=0), HB)
        (hs, na_s, nb_s), mixer_bf16 = _small_mixer(xs, ha, hb, params, l, layer_bf16)
        xs, ffn_bf16 = _small_ffn(hs, params, l, last, layer_bf16)
        sa.append(na_s[1:])
        sb.append(nb_s[1:])
        h, na_p, nb_p = _prompt_mixer(xp, _front_pad(na_s[0], HA), _front_pad(nb_s[0], HB), params, l,
                                      mixer_bf16)
        pa.append(na_p)
        pb.append(nb_p)
        xp, layer_bf16 = _prompt_ffn(h.reshape(b_p * seq, D), params, l, last, ffn_bf16,
                                     cast_layer=None if last else l + 1)
        xp = xp.reshape(b_p, seq, D)
        layer_bf16 = layer_bf16 or None

    y_prompt = xp
    y_sample = xs[N_META:].reshape(b_s, seq_s, D)
    return (y_prompt, y_sample, jnp.stack(pa, axis=0), jnp.stack(pb, axis=0),
            jnp.stack(sa, axis=0), jnp.stack(sb, axis=0))
```
